```python
import jax, jax.numpy as jnp
from jax import lax
import numpy as np

D_MODEL = 1024
BATCH = 2
SEQ = 8192
DEPTH = 2
DEC_BATCH = 128
DEC_SEQ = 4
PAST_LEN = 2048
PAGE_SIZE = 128

N_A_LAYERS = DEPTH // 2
N_B_LAYERS = DEPTH - N_A_LAYERS
SGU_CHUNK = 128
SGU_GROUPS = 8
SGU_GROUP_DIM = D_MODEL // SGU_GROUPS
N_HEADS = 16
HEAD_DIM = D_MODEL // N_HEADS
SB_BLOCK = 128
SB_BIAS_INIT = -8.0
D_FF = ((8 * D_MODEL // 3 + 255) // 256) * 256
NORM_EPS = 1e-6

kernel_name = 'yoco_sgu_stickbreaking_decode_step'


def rmsnorm(x, g):
    xf = x.astype(jnp.float32)
    r = lax.rsqrt(jnp.mean(xf * xf, axis=-1, keepdims=True) + NORM_EPS)
    return (xf * r).astype(x.dtype) * g


def swiglu(x, wg, wu, wd):
    return (jax.nn.silu(x @ wg) * (x @ wu)) @ wd


def sgu_mixer(xn, w_in, v_g, ws, bs, w_out, chunk_len):
    B, T, _ = xn.shape
    z = jax.nn.gelu(xn @ w_in, approximate=False)
    u, v = jnp.split(z, 2, axis=-1)
    v = rmsnorm(v, v_g)
    vc = v.reshape(B, T // chunk_len, chunk_len, SGU_GROUPS, SGU_GROUP_DIM)
    mask = jnp.tril(jnp.ones((chunk_len, chunk_len), dtype=bool))
    w = jnp.where(mask, ws[:, :chunk_len, :chunk_len], 0.0)
    s = jnp.einsum('gts,bnsgd->bntgd', w, vc) + bs[:, :chunk_len].T[None, None, :, :, None]
    y = (u * s.reshape(B, T, D_MODEL)) @ w_out
    return y, v


def stick_breaking(q, k, v, bias, q_pos, k_pos):
    z = jnp.einsum('bthd,bshd->bhts', q, k).astype(jnp.float32) * (HEAD_DIM ** -0.5) \
        + bias.astype(jnp.float32)[None, :, None, None]
    causal = k_pos[None, :] < q_pos[:, None]
    log_beta = jax.nn.log_sigmoid(z)
    log_rest = jnp.where(causal, jax.nn.log_sigmoid(-z), 0.0)
    after = lax.cumsum(log_rest, axis=3, reverse=True) - log_rest
    a = jnp.where(causal, jnp.exp(log_beta + after), 0.0)
    out = jnp.einsum('bhts,bshd->bthd', a, v.astype(jnp.float32))
    return out.astype(q.dtype)


def stick_breaking_prompt(q, k, v, bias):
    B, T, H, Dh = q.shape
    nb = T // SB_BLOCK
    qb = q.reshape(B, nb, SB_BLOCK, H, Dh).transpose(1, 0, 2, 3, 4)
    qpos = jnp.arange(T, dtype=jnp.int32).reshape(nb, SB_BLOCK)
    kpos = jnp.arange(T, dtype=jnp.int32)
    out = lax.map(lambda a: stick_breaking(a[0], k, v, bias, a[1], kpos), (qb, qpos))
    return out.transpose(1, 0, 2, 3, 4).reshape(B, T, H, Dh)


def shared_kv(h, kv_norm_g, w_k, w_v):
    B, T, _ = h.shape
    hn = rmsnorm(h, kv_norm_g)
    k = (hn @ w_k).reshape(B, T, N_HEADS, HEAD_DIM)
    v = (hn @ w_v).reshape(B, T, N_HEADS, HEAD_DIM)
    return k, v


def setup_inputs(seed: int = 0) -> dict:
    key = jax.random.key(seed)
    ks = jax.random.split(key, 24)
    n_pages = PAST_LEN // PAGE_SIZE
    n_used = DEC_BATCH * n_pages
    n_phys = (n_used * 5) // 4

    def w(k, shape, fan_in):
        return jax.random.normal(k, shape, jnp.float32) * (fan_in ** -0.5)

    def gain(k, shape):
        return 1.0 + 0.02 * jax.random.normal(k, shape, jnp.float32)

    perm = jax.random.permutation(ks[4], n_phys)[:n_used]
    page_table = perm.reshape(DEC_BATCH, n_pages).astype(jnp.int32)
    return {
        'x_prompt': jax.random.normal(ks[0], (BATCH, SEQ, D_MODEL), jnp.float32),
        'x_sample': jax.random.normal(ks[1], (DEC_BATCH, DEC_SEQ, D_MODEL), jnp.float32),
        'cache_k': jax.random.normal(ks[2], (n_phys, PAGE_SIZE, N_HEADS, HEAD_DIM), jnp.float32),
        'cache_v': jax.random.normal(ks[3], (n_phys, PAGE_SIZE, N_HEADS, HEAD_DIM), jnp.float32),
        'page_table': page_table,
        'norm_mix_g': gain(ks[5], (DEPTH, D_MODEL)),
        'a_w_in': w(ks[6], (N_A_LAYERS, D_MODEL, 2 * D_MODEL), D_MODEL),
        'a_v_g': gain(ks[7], (N_A_LAYERS, D_MODEL)),
        'a_ws': w(ks[8], (N_A_LAYERS, SGU_GROUPS, SGU_CHUNK, SGU_CHUNK), SGU_CHUNK),
        'a_bs': 1.0 + 0.02 * jax.random.normal(ks[9], (N_A_LAYERS, SGU_GROUPS, SGU_CHUNK), jnp.float32),
        'a_w_out': w(ks[10], (N_A_LAYERS, D_MODEL, D_MODEL), D_MODEL),
        'kv_norm_g': gain(ks[11], (D_MODEL,)),
        'w_k': w(ks[12], (D_MODEL, D_MODEL), D_MODEL),
        'w_v': w(ks[13], (D_MODEL, D_MODEL), D_MODEL),
        'b_w_q': w(ks[14], (N_B_LAYERS, D_MODEL, D_MODEL), D_MODEL),
        'b_sb_bias': SB_BIAS_INIT + 0.1 * jax.random.normal(ks[21], (N_B_LAYERS, N_HEADS), jnp.float32),
        'b_w_o': w(ks[15], (N_B_LAYERS, D_MODEL, D_MODEL), D_MODEL),
        'norm_ffn_g': gain(ks[16], (DEPTH, D_MODEL)),
        'w_gate': w(ks[17], (DEPTH, D_MODEL, D_FF), D_MODEL),
        'w_up': w(ks[18], (DEPTH, D_MODEL, D_FF), D_MODEL),
        'w_down': w(ks[19], (DEPTH, D_FF, D_MODEL), D_FF),
        'norm_final_g': gain(ks[20], (D_MODEL,)),
    }


def reference(x_prompt, x_sample, cache_k, cache_v, page_table, norm_mix_g, a_w_in, a_v_g, a_ws, a_bs,
              a_w_out, kv_norm_g, w_k, w_v, b_w_q, b_sb_bias, b_w_o, norm_ffn_g, w_gate, w_up, w_down,
              norm_final_g):
    B, T, _ = x_prompt.shape
    DB, L, _ = x_sample.shape
    past = page_table.shape[1] * cache_k.shape[1]
    h_p, h_s = x_prompt, x_sample
    sgu_states = []
    for layer in range(DEPTH):
        if layer < N_A_LAYERS:
            i = layer
            yp, _ = sgu_mixer(rmsnorm(h_p, norm_mix_g[layer]), a_w_in[i], a_v_g[i], a_ws[i], a_bs[i],
                              a_w_out[i], SGU_CHUNK)
            ys, v_rows = sgu_mixer(rmsnorm(h_s, norm_mix_g[layer]), a_w_in[i], a_v_g[i], a_ws[i], a_bs[i],
                                   a_w_out[i], L)
            sgu_states.append(v_rows)
        else:
            if layer == N_A_LAYERS:
                k_p, v_p = shared_kv(h_p, kv_norm_g, w_k, w_v)
                k_s, v_s = shared_kv(h_s, kv_norm_g, w_k, w_v)
                k_past = cache_k[page_table].reshape(DB, past, N_HEADS, HEAD_DIM)
                v_past = cache_v[page_table].reshape(DB, past, N_HEADS, HEAD_DIM)
                k_all = jnp.concatenate([k_past, k_s], axis=1)
                v_all = jnp.concatenate([v_past, v_s], axis=1)
                kpos_s = jnp.arange(past + L, dtype=jnp.int32)
                qpos_s = past + jnp.arange(L, dtype=jnp.int32)
            j = layer - N_A_LAYERS
            q_p = (rmsnorm(h_p, norm_mix_g[layer]) @ b_w_q[j]).reshape(B, T, N_HEADS, HEAD_DIM)
            yp = stick_breaking_prompt(q_p, k_p, v_p, b_sb_bias[j]).reshape(B, T, D_MODEL) @ b_w_o[j]
            q_s = (rmsnorm(h_s, norm_mix_g[layer]) @ b_w_q[j]).reshape(DB, L, N_HEADS, HEAD_DIM)
            ys = stick_breaking(q_s, k_all, v_all, b_sb_bias[j], qpos_s, kpos_s).reshape(DB, L, D_MODEL) @ b_w_o[j]
        h_p = h_p + yp
        h_s = h_s + ys
        h_p = h_p + swiglu(rmsnorm(h_p, norm_ffn_g[layer]), w_gate[layer], w_up[layer], w_down[layer])
        h_s = h_s + swiglu(rmsnorm(h_s, norm_ffn_g[layer]), w_gate[layer], w_up[layer], w_down[layer])
    y_prompt = rmsnorm(h_p, norm_final_g)
    y_sample = rmsnorm(h_s, norm_final_g)
    sgu_v_sample = jnp.stack(sgu_states, axis=0)
    return (y_prompt, y_sample, k_p, v_p, k_s, v_s, sgu_v_sample)
```

```python
import functools
import math

import jax
import jax.numpy as jnp
from jax import lax
from jax.experimental import pallas as pl
from jax.experimental.pallas import tpu as pltpu

F32 = jnp.float32
BF16 = jnp.bfloat16

NORM_EPS = 1e-6
SGU_CHUNK = 128
SGU_GROUPS = 8
N_HEADS = 16
HEAD_DIM = 64
HEADS_PER_BLOCK = 2
DEC_GROUP = 4
V7X_VMEM_LIMIT = 56 * 1024 * 1024

ROW_TILE = 256
ATTN_TILE = 256
DEC_PAGES_PER_STEP = 4


def _const_spec(shape):
    nd = len(shape)
    return pl.BlockSpec(shape, lambda *_: (0,) * nd, pipeline_mode=pl.Buffered(1))


def _params(n_axes):
    return pltpu.CompilerParams(dimension_semantics=("arbitrary",) * n_axes, vmem_limit_bytes=V7X_VMEM_LIMIT)


def _rms_scale(x):
    return x * lax.rsqrt(jnp.mean(x * x, axis=-1, keepdims=True) + NORM_EPS)


def _gelu_exact(x):
    return 0.5 * x * (1.0 + lax.erf(x * math.sqrt(0.5)))


def _softplus(z):
    return jnp.maximum(z, 0.0) + jnp.log(1.0 + jnp.exp(-jnp.abs(z)))


def _nt_dot(a, b):
    return lax.dot_general(a, b, (((1,), (1,)), ((), ())), preferred_element_type=F32)


def _sgu_kernel(x_ref, gmix_ref, win_ref, vg_ref, mix_ref, bsb_ref, wout_ref, *rest, emit_v):
    if emit_v:
        h_ref, v_ref, gated_ref = rest
    else:
        h_ref, gated_ref = rest
    d = x_ref.shape[1]
    x = x_ref[...]
    xn = (_rms_scale(x) * gmix_ref[...]).astype(BF16)
    z = _gelu_exact(jnp.dot(xn, win_ref[...], preferred_element_type=F32))
    u = z[:, :d]
    vn = _rms_scale(z[:, d:]) * vg_ref[...]
    if emit_v:
        v_ref[...] = vn
    vb = vn.astype(BF16)
    row = lax.broadcasted_iota(jnp.int32, (SGU_CHUNK, SGU_CHUNK), 0)
    col = lax.broadcasted_iota(jnp.int32, (SGU_CHUNK, SGU_CHUNK), 1)
    tril = row >= col
    gd = d // SGU_GROUPS
    for g in range(SGU_GROUPS):
        w = jnp.where(tril, mix_ref[g], 0.0).astype(BF16)
        cs = slice(g * gd, (g + 1) * gd)
        for c in range(x.shape[0] // SGU_CHUNK):
            rs = slice(c * SGU_CHUNK, (c + 1) * SGU_CHUNK)
            s = jnp.dot(w, vb[rs, cs], preferred_element_type=F32) + bsb_ref[:, cs]
            gated_ref[rs, cs] = (u[rs, cs] * s).astype(BF16)
    h_ref[...] = x + jnp.dot(gated_ref[...], wout_ref[...], preferred_element_type=F32)


def _sgu_call(x, gmix, w_in, v_g, mix, bsb, w_out, *, emit_v):
    n, d = x.shape
    tm = ROW_TILE
    row_spec = pl.BlockSpec((tm, d), lambda i: (i, 0))
    out_shape = [jax.ShapeDtypeStruct((n, d), F32)]
    out_specs = [row_spec]
    if emit_v:
        out_shape.append(jax.ShapeDtypeStruct((n, d), F32))
        out_specs.append(row_spec)
    return pl.pallas_call(
        functools.partial(_sgu_kernel, emit_v=emit_v),
        grid=(n // tm,),
        in_specs=[row_spec, _const_spec((1, d)), _const_spec(w_in.shape), _const_spec((1, d)),
                  _const_spec(mix.shape), _const_spec(bsb.shape), _const_spec(w_out.shape)],
        out_specs=out_specs,
        out_shape=out_shape,
        scratch_shapes=[pltpu.VMEM((tm, d), BF16)],
        compiler_params=_params(1),
        name="sgu_emit_v" if emit_v else "sgu",
    )(x, gmix, w_in, v_g, mix, bsb, w_out)


def _ffn_kernel(*refs, with_proj, final_norm):
    refs = list(refs)
    h_ref = refs.pop(0)
    if with_proj:
        o_ref = refs.pop(0)
        wo_ref = refs.pop(0)
    g_ref, wg_ref, wu_ref, wd_ref = refs[:4]
    refs = refs[4:]
    if final_norm:
        gf_ref = refs.pop(0)
    (out_ref,) = refs
    h = h_ref[...]
    if with_proj:
        h = h + jnp.dot(o_ref[...], wo_ref[...], preferred_element_type=F32)
    xn = (_rms_scale(h) * g_ref[...]).astype(BF16)
    gate = jnp.dot(xn, wg_ref[...], preferred_element_type=F32)
    up = jnp.dot(xn, wu_ref[...], preferred_element_type=F32)
    act = (gate * jax.nn.sigmoid(gate) * up).astype(BF16)
    h = h + jnp.dot(act, wd_ref[...], preferred_element_type=F32)
    if final_norm:
        h = _rms_scale(h) * gf_ref[...]
    out_ref[...] = h


def _ffn_call(h, o, w_o, g, w_gate, w_up, w_down, g_final):
    n, d = h.shape
    tm = ROW_TILE
    with_proj = o is not None
    final_norm = g_final is not None
    row_spec = pl.BlockSpec((tm, d), lambda i: (i, 0))
    args, in_specs = [h], [row_spec]
    if with_proj:
        args += [o, w_o]
        in_specs += [row_spec, _const_spec(w_o.shape)]
    args += [g, w_gate, w_up, w_down]
    in_specs += [_const_spec((1, d)), _const_spec(w_gate.shape), _const_spec(w_up.shape),
                 _const_spec(w_down.shape)]
    if final_norm:
        args.append(g_final)
        in_specs.append(_const_spec((1, d)))
    return pl.pallas_call(
        functools.partial(_ffn_kernel, with_proj=with_proj, final_norm=final_norm),
        grid=(n // tm,),
        in_specs=in_specs,
        out_specs=row_spec,
        out_shape=jax.ShapeDtypeStruct((n, d), F32),
        compiler_params=_params(1),
        name="ffn_proj_final" if with_proj else "ffn",
    )(*args)


def _kvq_rows_kernel(h_ref, gkv_ref, gq_ref, wk_ref, wv_ref, wq_ref, k_ref, v_ref, q_ref):
    xr = _rms_scale(h_ref[...])
    hkv = (xr * gkv_ref[...]).astype(BF16)
    hq = (xr * gq_ref[...]).astype(BF16)
    k_ref[...] = jnp.dot(hkv, wk_ref[...], preferred_element_type=F32)
    v_ref[...] = jnp.dot(hkv, wv_ref[...], preferred_element_type=F32)
    q_ref[...] = jnp.dot(hq, wq_ref[...], preferred_element_type=F32) * (HEAD_DIM ** -0.5)


def _kvq_rows_call(h, g_kv, g_q, w_k, w_v, w_q):
    n, d = h.shape
    tm = ROW_TILE
    row_spec = pl.BlockSpec((tm, d), lambda i: (i, 0))
    wspec = _const_spec(w_k.shape)
    gspec = _const_spec((1, d))
    return pl.pallas_call(
        _kvq_rows_kernel,
        grid=(n // tm,),
        in_specs=[row_spec, gspec, gspec, wspec, wspec, wspec],
        out_specs=[row_spec] * 3,
        out_shape=[jax.ShapeDtypeStruct((n, d), F32)] * 3,
        compiler_params=_params(1),
        name="kvq_rows",
    )(h, g_kv, g_q, w_k, w_v, w_q)


def _kvq_cols_kernel(h_ref, gkv_ref, gq_ref, wkt_ref, wvt_ref, wq_ref, kt_ref, vt_ref, ktb_ref, vtb_ref, qb_ref):
    xr = _rms_scale(h_ref[0])
    hkv = (xr * gkv_ref[...]).astype(BF16)
    hq = (xr * gq_ref[...]).astype(BF16)
    kt = _nt_dot(wkt_ref[...], hkv)
    vt = _nt_dot(wvt_ref[...], hkv)
    kt_ref[0] = kt
    vt_ref[0] = vt
    ktb_ref[0] = kt.astype(BF16)
    vtb_ref[0] = vt.astype(BF16)
    q = jnp.dot(hq, wq_ref[...], preferred_element_type=F32)
    qb_ref[0] = (q * (HEAD_DIM ** -0.5)).astype(BF16)


def _kvq_cols_call(h, g_kv, g_q, w_kt, w_vt, w_q):
    b, t_len, d = h.shape
    tm = ROW_TILE
    row_spec = pl.BlockSpec((1, tm, d), lambda bi, i: (bi, i, 0))
    col_spec = pl.BlockSpec((1, d, tm), lambda bi, i: (bi, 0, i))
    wspec = _const_spec(w_q.shape)
    gspec = _const_spec((1, d))
    return pl.pallas_call(
        _kvq_cols_kernel,
        grid=(b, t_len // tm),
        in_specs=[row_spec, gspec, gspec, wspec, wspec, wspec],
        out_specs=[col_spec] * 4 + [row_spec],
        out_shape=[jax.ShapeDtypeStruct((b, d, t_len), F32)] * 2
        + [jax.ShapeDtypeStruct((b, d, t_len), BF16)] * 2 + [jax.ShapeDtypeStruct((b, t_len, d), BF16)],
        compiler_params=_params(2),
        name="kvq_cols",
    )(h, g_kv, g_q, w_kt, w_vt, w_q)


def _sb_tile(qm, kt, vt, bias, tri, carry, causal):
    z = jnp.dot(qm, kt, preferred_element_type=F32) + bias
    sp = _softplus(z)
    if causal is not None:
        sp = jnp.where(causal, sp, 0.0)
    cum = jnp.dot(sp.astype(BF16), tri, preferred_element_type=F32)
    a = jnp.exp(z - cum - carry)
    if causal is not None:
        a = jnp.where(causal, a, 0.0)
    return _nt_dot(a.astype(BF16), vt), carry + cum[:, :1]


def _attn_kernel(bias_ref, q_ref, kt_ref, vt_ref, o_ref, acc_ref, carry_ref):
    t = ATTN_TILE
    hp = pl.program_id(1)
    i = pl.program_id(2)
    q = q_ref[0]
    lane = lax.broadcasted_iota(jnp.int32, (1, HEADS_PER_BLOCK * HEAD_DIM), 1)
    row = lax.broadcasted_iota(jnp.int32, (t, t), 0)
    col = lax.broadcasted_iota(jnp.int32, (t, t), 1)
    tri = (row >= col).astype(BF16)
    strictly_below = col < row
    result = jnp.zeros((t, HEADS_PER_BLOCK * HEAD_DIM), F32)
    for hh in range(HEADS_PER_BLOCK):
        head_lanes = (lane >= hh * HEAD_DIM) & (lane < (hh + 1) * HEAD_DIM)
        bias = bias_ref[hp * HEADS_PER_BLOCK + hh]
        qm = jnp.where(head_lanes, q, jnp.zeros_like(q))
        start = pl.multiple_of(i * t, t)
        out, carry = _sb_tile(qm, kt_ref[0, :, pl.ds(start, t)], vt_ref[0, :, pl.ds(start, t)], bias, tri,
                              jnp.zeros((t, 1), F32), strictly_below)
        acc_ref[...] = out
        carry_ref[...] = carry

        def body(step, _):
            start = pl.multiple_of((i - 1 - step) * t, t)
            out, carry = _sb_tile(qm, kt_ref[0, :, pl.ds(start, t)], vt_ref[0, :, pl.ds(start, t)], bias, tri,
                                  carry_ref[...], None)
            acc_ref[...] += out
            carry_ref[...] = carry
            return 0

        lax.fori_loop(0, i, body, 0)
        result = jnp.where(head_lanes, acc_ref[...], result)
    o_ref[0] = result.astype(o_ref.dtype)


def _attn_call(qb, ktb, vtb, bias):
    b, t_len, d = qb.shape
    t = ATTN_TILE
    w = HEADS_PER_BLOCK * HEAD_DIM
    q_spec = pl.BlockSpec((1, t, w), lambda bi, hp, i: (bi, i, hp))
    kv_spec = pl.BlockSpec((1, w, t_len), lambda bi, hp, i: (bi, hp, 0))
    return pl.pallas_call(
        _attn_kernel,
        grid=(b, d // w, t_len // t),
        in_specs=[pl.BlockSpec(memory_space=pltpu.SMEM), q_spec, kv_spec, kv_spec],
        out_specs=q_spec,
        out_shape=jax.ShapeDtypeStruct((b, t_len, d), BF16),
        scratch_shapes=[pltpu.VMEM((t, w), F32), pltpu.VMEM((t, 1), F32)],
        compiler_params=_params(3),
        name="sb_prompt",
    )(bias, qb, ktb, vtb)


def _dec_kernel(pt_ref, qbd_ref, bias_ref, knew_ref, vnew_ref, *rest, n_new):
    del pt_ref
    pages = DEC_PAGES_PER_STEP
    k_refs, v_refs = rest[:pages], rest[pages:2 * pages]
    o_ref, acc_ref, carry_ref = rest[2 * pages:]
    p = pl.program_id(1)
    gw = DEC_GROUP * HEAD_DIM
    gr = DEC_GROUP * n_new
    n_groups = N_HEADS // DEC_GROUP
    page = k_refs[0].shape[3]
    bias = bias_ref[...]
    r16 = lax.broadcasted_iota(jnp.int32, (gr, 1), 0)
    own = (lax.broadcasted_iota(jnp.int32, (gr, gw), 0) // n_new
           == lax.broadcasted_iota(jnp.int32, (gr, gw), 1) // HEAD_DIM)

    @pl.when(p == 0)
    def _():
        for g in range(n_groups):
            rs = slice(g * gr, (g + 1) * gr)
            cs = slice(g * gw, (g + 1) * gw)
            qg = qbd_ref[0, rs, :].astype(F32)
            zs, sps = [], []
            for j in range(n_new):
                z = jnp.sum(qg * knew_ref[0, j:j + 1, cs], axis=1, keepdims=True) + bias[rs]
                zs.append(z)
                sps.append(jnp.where(j < r16 % n_new, _softplus(z), 0.0))
            cum = jnp.zeros((gr, 1), F32)
            acc = jnp.zeros((gr, gw), F32)
            for j in reversed(range(n_new)):
                cum = cum + sps[j]
                a = jnp.where(j < r16 % n_new, jnp.exp(zs[j] - cum), 0.0)
                acc = acc + a * vnew_ref[0, j:j + 1, cs]
            acc_ref[rs, :] = acc
            carry_ref[rs, :] = cum

    jj = lax.broadcasted_iota(jnp.int32, (page, page), 0)
    ss = lax.broadcasted_iota(jnp.int32, (page, page), 1)
    tri = (jj >= ss).astype(BF16)
    for k_ref, v_ref in zip(k_refs, v_refs):
        kp = k_ref[0].reshape(N_HEADS * HEAD_DIM, page)
        vp = v_ref[0].reshape(N_HEADS * HEAD_DIM, page)
        z = jnp.concatenate(
            [jnp.dot(qbd_ref[0, g * gr:(g + 1) * gr, :], kp[g * gw:(g + 1) * gw].astype(BF16),
                     preferred_element_type=F32) for g in range(n_groups)], axis=0) + bias
        sp = _softplus(z)
        cum = jnp.dot(sp.astype(BF16), tri, preferred_element_type=F32)
        a = jnp.exp(z - cum - carry_ref[...]).astype(BF16)
        for g in range(n_groups):
            rs = slice(g * gr, (g + 1) * gr)
            acc_ref[rs, :] += _nt_dot(a[rs], vp[g * gw:(g + 1) * gw].astype(BF16))
        carry_ref[...] = carry_ref[...] + cum[:, :1]

    @pl.when(p == pl.num_programs(1) - 1)
    def _():
        for g in range(n_groups):
            mine = jnp.where(own, acc_ref[g * gr:(g + 1) * gr, :], 0.0)
            o = mine[0:n_new]
            for h4 in range(1, DEC_GROUP):
                o = o + mine[h4 * n_new:(h4 + 1) * n_new]
            o_ref[0, :, g * gw:(g + 1) * gw] = o.astype(o_ref.dtype)


def _dec_call(page_table, qbd, bias_col, k_new, v_new, cache_kt, cache_vt):
    db, nr, gw = qbd.shape
    n_new, d = k_new.shape[1:]
    n_pages = page_table.shape[1]
    pages = DEC_PAGES_PER_STEP
    pt = page_table.reshape(-1)
    seq_spec = lambda rows, cols: pl.BlockSpec((1, rows, cols), lambda b, p, pt_ref: (b, 0, 0))

    def page_spec(k):
        return pl.BlockSpec((1,) + cache_kt.shape[1:],
                            lambda b, p, pt_ref: (pt_ref[b * n_pages + n_pages - 1 - p * pages - k], 0, 0, 0))

    grid_spec = pltpu.PrefetchScalarGridSpec(
        num_scalar_prefetch=1,
        grid=(db, n_pages // pages),
        in_specs=[seq_spec(nr, gw), pl.BlockSpec((nr, 1), lambda b, p, pt_ref: (0, 0)),
                  seq_spec(n_new, d), seq_spec(n_new, d)]
        + [page_spec(k) for k in range(pages)] * 2,
        out_specs=seq_spec(n_new, d),
        scratch_shapes=[pltpu.VMEM((nr, gw), F32), pltpu.VMEM((nr, 1), F32)],
    )
    return pl.pallas_call(
        functools.partial(_dec_kernel, n_new=n_new),
        grid_spec=grid_spec,
        out_shape=jax.ShapeDtypeStruct((db, n_new, d), BF16),
        compiler_params=_params(2),
        name="sb_decode",
    )(pt, qbd, bias_col, k_new, v_new, *([cache_kt] * pages), *([cache_vt] * pages))


def kernel(x_prompt, x_sample, cache_k, cache_v, page_table, norm_mix_g, a_w_in, a_v_g, a_ws, a_bs, a_w_out,
           kv_norm_g, w_k, w_v, b_w_q, b_sb_bias, b_w_o, norm_ffn_g, w_gate, w_up, w_down, norm_final_g):
    bsz, seq, d = x_prompt.shape
    db, n_l, _ = x_sample.shape
    depth = norm_mix_g.shape[0]
    n_a = a_w_in.shape[0]
    n_pages = page_table.shape[1]
    assert depth == 2 and n_a == 1 and b_w_q.shape[0] == 1, "one SGU layer followed by one stick-breaking layer"
    assert d == N_HEADS * HEAD_DIM and SGU_CHUNK % n_l == 0 and n_pages % DEC_PAGES_PER_STEP == 0

    row = lambda g: g.reshape(1, d)
    bf = lambda w: w.astype(BF16)

    gd = d // SGU_GROUPS
    mix_p = a_ws[0]
    bsb_p = jnp.repeat(a_bs[0].T, gd, axis=1)
    reps = SGU_CHUNK // n_l
    eye = jnp.eye(reps, dtype=F32)
    mix_s = jnp.einsum('ab,gts->gatbs', eye, a_ws[0][:, :n_l, :n_l]).reshape(SGU_GROUPS, SGU_CHUNK, SGU_CHUNK)
    bsb_s = jnp.repeat(jnp.tile(a_bs[0][:, :n_l], (1, reps)).T, gd, axis=1)

    w_in, w_out = bf(a_w_in[0]), bf(a_w_out[0])
    xp = x_prompt.reshape(bsz * seq, d)
    xs = x_sample.reshape(db * n_l, d)
    (hp,) = _sgu_call(xp, row(norm_mix_g[0]), w_in, row(a_v_g[0]), mix_p, bsb_p, w_out, emit_v=False)
    hs, v_rows = _sgu_call(xs, row(norm_mix_g[0]), w_in, row(a_v_g[0]), mix_s, bsb_s, w_out, emit_v=True)
    wg0, wu0, wd0 = bf(w_gate[0]), bf(w_up[0]), bf(w_down[0])
    hp = _ffn_call(hp, None, None, row(norm_ffn_g[0]), wg0, wu0, wd0, None)
    hs = _ffn_call(hs, None, None, row(norm_ffn_g[0]), wg0, wu0, wd0, None)

    wq, wo = bf(b_w_q[0]), bf(b_w_o[0])
    kt_p, vt_p, ktb_p, vtb_p, qb_p = _kvq_cols_call(hp.reshape(bsz, seq, d), row(kv_norm_g), row(norm_mix_g[1]),
                                                    bf(w_k.T), bf(w_v.T), wq)
    k_s, v_s, q_s = _kvq_rows_call(hs, row(kv_norm_g), row(norm_mix_g[1]), bf(w_k), bf(w_v), wq)

    bias = b_sb_bias[0].astype(F32)
    o_p = _attn_call(qb_p, ktb_p, vtb_p, bias)

    n_groups = N_HEADS // DEC_GROUP
    q5 = q_s.reshape(db, n_l, n_groups, DEC_GROUP, HEAD_DIM)
    qbd = jnp.einsum('blghd,hk->bghlkd', q5, jnp.eye(DEC_GROUP, dtype=F32))
    qbd = qbd.reshape(db, N_HEADS * n_l, DEC_GROUP * HEAD_DIM).astype(BF16)
    bias_col = jnp.repeat(bias, n_l).reshape(N_HEADS * n_l, 1)
    cache_kt = jnp.transpose(cache_k, (0, 2, 3, 1))
    cache_vt = jnp.transpose(cache_v, (0, 2, 3, 1))
    o_s = _dec_call(page_table, qbd, bias_col, k_s.reshape(db, n_l, d), v_s.reshape(db, n_l, d),
                    cache_kt, cache_vt)

    wg1, wu1, wd1 = bf(w_gate[1]), bf(w_up[1]), bf(w_down[1])
    y_p = _ffn_call(hp, o_p.reshape(bsz * seq, d), wo, row(norm_ffn_g[1]), wg1, wu1, wd1, row(norm_final_g))
    y_s = _ffn_call(hs, o_s.reshape(db * n_l, d), wo, row(norm_ffn_g[1]), wg1, wu1, wd1, row(norm_final_g))

    heads = (N_HEADS, HEAD_DIM)
    to_bthd = lambda xt: jnp.transpose(xt.reshape(bsz, *heads, seq), (0, 3, 1, 2))
    return (y_p.reshape(bsz, seq, d), y_s.reshape(db, n_l, d),
            to_bthd(kt_p), to_bthd(vt_p),
            k_s.reshape(db, n_l, *heads), v_s.reshape(db, n_l, *heads),
            v_rows.reshape(n_a, db, n_l, d))
```

```python
import functools
import math

import jax
import jax.numpy as jnp
from jax import lax
from jax.experimental import pallas as pl
from jax.experimental.pallas import tpu as pltpu

F32 = jnp.float32
BF16 = jnp.bfloat16

NORM_EPS = 1e-6
LOG2E = math.log2(math.e)
SGU_CHUNK = 128
SGU_GROUPS = 8
N_HEADS = 16
HEAD_DIM = 64
HEADS_PER_BLOCK = 4
DEC_GROUP = 4
V7X_VMEM_LIMIT = 56 * 1024 * 1024

ROW_TILE = 256
ATTN_TQ = 512
ATTN_TK = 256
MASKED_SCORE = -1e30
DEC_PAGES_PER_STEP = 8


def _const_spec(shape):
    nd = len(shape)
    return pl.BlockSpec(shape, lambda *_: (0,) * nd, pipeline_mode=pl.Buffered(1))


def _params(n_axes):
    return pltpu.CompilerParams(dimension_semantics=("arbitrary",) * n_axes, vmem_limit_bytes=V7X_VMEM_LIMIT)


def _rms_scale(x):
    return x * lax.rsqrt(jnp.mean(x * x, axis=-1, keepdims=True) + NORM_EPS)


def _gelu_exact(x):
    return 0.5 * x * (1.0 + lax.erf(x * math.sqrt(0.5)))


def _softplus(z):
    return jnp.maximum(z, 0.0) + jnp.log(1.0 + jnp.exp(-jnp.abs(z)))


def _nt_dot(a, b):
    return lax.dot_general(a, b, (((1,), (1,)), ((), ())), preferred_element_type=F32)


def _sgu_kernel(x_ref, gmix_ref, win_ref, vg_ref, mix_ref, bsb_ref, wout_ref, *rest, emit_v):
    if emit_v:
        h_ref, v_ref, gated_ref = rest
    else:
        h_ref, gated_ref = rest
    d = x_ref.shape[1]
    x = x_ref[...]
    xn = (_rms_scale(x) * gmix_ref[...]).astype(BF16)
    z = _gelu_exact(jnp.dot(xn, win_ref[...], preferred_element_type=F32))
    u = z[:, :d]
    vn = _rms_scale(z[:, d:]) * vg_ref[...]
    if emit_v:
        v_ref[...] = vn
    vb = vn.astype(BF16)
    row = lax.broadcasted_iota(jnp.int32, (SGU_CHUNK, SGU_CHUNK), 0)
    col = lax.broadcasted_iota(jnp.int32, (SGU_CHUNK, SGU_CHUNK), 1)
    tril = row >= col
    gd = d // SGU_GROUPS
    for g in range(SGU_GROUPS):
        w = jnp.where(tril, mix_ref[g], 0.0).astype(BF16)
        cs = slice(g * gd, (g + 1) * gd)
        for c in range(x.shape[0] // SGU_CHUNK):
            rs = slice(c * SGU_CHUNK, (c + 1) * SGU_CHUNK)
            s = jnp.dot(w, vb[rs, cs], preferred_element_type=F32) + bsb_ref[:, cs]
            gated_ref[rs, cs] = (u[rs, cs] * s).astype(BF16)
    h_ref[...] = x + jnp.dot(gated_ref[...], wout_ref[...], preferred_element_type=F32)


def _sgu_call(x, gmix, w_in, v_g, mix, bsb, w_out, *, emit_v):
    n, d = x.shape
    tm = ROW_TILE
    row_spec = pl.BlockSpec((tm, d), lambda i: (i, 0))
    out_shape = [jax.ShapeDtypeStruct((n, d), F32)]
    out_specs = [row_spec]
    if emit_v:
        out_shape.append(jax.ShapeDtypeStruct((n, d), F32))
        out_specs.append(row_spec)
    return pl.pallas_call(
        functools.partial(_sgu_kernel, emit_v=emit_v),
        grid=(n // tm,),
        in_specs=[row_spec, _const_spec((1, d)), _const_spec(w_in.shape), _const_spec((1, d)),
                  _const_spec(mix.shape), _const_spec(bsb.shape), _const_spec(w_out.shape)],
        out_specs=out_specs,
        out_shape=out_shape,
        scratch_shapes=[pltpu.VMEM((tm, d), BF16)],
        compiler_params=_params(1),
        name="sgu_emit_v" if emit_v else "sgu",
    )(x, gmix, w_in, v_g, mix, bsb, w_out)


def _ffn_kernel(*refs, with_proj, final_norm):
    refs = list(refs)
    h_ref = refs.pop(0)
    if with_proj:
        o_ref = refs.pop(0)
        wo_ref = refs.pop(0)
    g_ref, wg_ref, wu_ref, wd_ref = refs[:4]
    refs = refs[4:]
    if final_norm:
        gf_ref = refs.pop(0)
    (out_ref,) = refs
    h = h_ref[...]
    if with_proj:
        h = h + jnp.dot(o_ref[...], wo_ref[...], preferred_element_type=F32)
    xn = (_rms_scale(h) * g_ref[...]).astype(BF16)
    gate = jnp.dot(xn, wg_ref[...], preferred_element_type=F32)
    up = jnp.dot(xn, wu_ref[...], preferred_element_type=F32)
    act = (gate * jax.nn.sigmoid(gate) * up).astype(BF16)
    h = h + jnp.dot(act, wd_ref[...], preferred_element_type=F32)
    if final_norm:
        h = _rms_scale(h) * gf_ref[...]
    out_ref[...] = h


def _ffn_call(h, o, w_o, g, w_gate, w_up, w_down, g_final):
    n, d = h.shape
    tm = ROW_TILE
    with_proj = o is not None
    final_norm = g_final is not None
    row_spec = pl.BlockSpec((tm, d), lambda i: (i, 0))
    args, in_specs = [h], [row_spec]
    if with_proj:
        args += [o, w_o]
        in_specs += [row_spec, _const_spec(w_o.shape)]
    args += [g, w_gate, w_up, w_down]
    in_specs += [_const_spec((1, d)), _const_spec(w_gate.shape), _const_spec(w_up.shape),
                 _const_spec(w_down.shape)]
    if final_norm:
        args.append(g_final)
        in_specs.append(_const_spec((1, d)))
    return pl.pallas_call(
        functools.partial(_ffn_kernel, with_proj=with_proj, final_norm=final_norm),
        grid=(n // tm,),
        in_specs=in_specs,
        out_specs=row_spec,
        out_shape=jax.ShapeDtypeStruct((n, d), F32),
        compiler_params=_params(1),
        name="ffn_proj_final" if with_proj else "ffn",
    )(*args)


def _kvq_rows_kernel(h_ref, gkv_ref, gq_ref, wk_ref, wv_ref, wq_ref, k_ref, v_ref, q_ref):
    xr = _rms_scale(h_ref[...])
    hkv = (xr * gkv_ref[...]).astype(BF16)
    hq = (xr * gq_ref[...]).astype(BF16)
    k_ref[...] = jnp.dot(hkv, wk_ref[...], preferred_element_type=F32)
    v_ref[...] = jnp.dot(hkv, wv_ref[...], preferred_element_type=F32)
    q_ref[...] = jnp.dot(hq, wq_ref[...], preferred_element_type=F32) * (HEAD_DIM ** -0.5)


def _kvq_rows_call(h, g_kv, g_q, w_k, w_v, w_q):
    n, d = h.shape
    tm = ROW_TILE
    row_spec = pl.BlockSpec((tm, d), lambda i: (i, 0))
    wspec = _const_spec(w_k.shape)
    gspec = _const_spec((1, d))
    return pl.pallas_call(
        _kvq_rows_kernel,
        grid=(n // tm,),
        in_specs=[row_spec, gspec, gspec, wspec, wspec, wspec],
        out_specs=[row_spec] * 3,
        out_shape=[jax.ShapeDtypeStruct((n, d), F32)] * 3,
        compiler_params=_params(1),
        name="kvq_rows",
    )(h, g_kv, g_q, w_k, w_v, w_q)


def _kvq_cols_kernel(h_ref, gkv_ref, gq_ref, wk_ref, wkt_ref, wvt_ref, wqt_ref,
                     kt_ref, vt_ref, kb_ref, vtb_ref, qtb_ref):
    xr = _rms_scale(h_ref[0])
    hkv = (xr * gkv_ref[...]).astype(BF16)
    hq = (xr * gq_ref[...]).astype(BF16)
    kt_ref[0] = _nt_dot(wkt_ref[...], hkv)
    vt = _nt_dot(wvt_ref[...], hkv)
    vt_ref[0] = vt
    vtb_ref[0] = vt.astype(BF16)
    kb_ref[0] = jnp.dot(hkv, wk_ref[...], preferred_element_type=F32).astype(BF16)
    qtb_ref[0] = (_nt_dot(wqt_ref[...], hq) * (HEAD_DIM ** -0.5)).astype(BF16)


def _kvq_cols_call(h, g_kv, g_q, w_k, w_kt, w_vt, w_qt):
    b, t_len, d = h.shape
    tm = ROW_TILE
    row_spec = pl.BlockSpec((1, tm, d), lambda bi, i: (bi, i, 0))
    col_spec = pl.BlockSpec((1, d, tm), lambda bi, i: (bi, 0, i))
    wspec = _const_spec(w_k.shape)
    gspec = _const_spec((1, d))
    cols = lambda dt: jax.ShapeDtypeStruct((b, d, t_len), dt)
    return pl.pallas_call(
        _kvq_cols_kernel,
        grid=(b, t_len // tm),
        in_specs=[row_spec, gspec, gspec, wspec, wspec, wspec, wspec],
        out_specs=[col_spec, col_spec, row_spec, col_spec, col_spec],
        out_shape=[cols(F32), cols(F32), jax.ShapeDtypeStruct((b, t_len, d), BF16), cols(BF16), cols(BF16)],
        compiler_params=_params(2),
        name="kvq_cols",
    )(h, g_kv, g_q, w_k, w_kt, w_vt, w_qt)


def _neg_abs(x):
    return pltpu.bitcast(pltpu.bitcast(x, jnp.int32) | jnp.int32(-2 ** 31), F32)


def _attn_kernel(bias_ref, qt_ref, k_ref, vt_ref, o_ref, acc_ref, z_ref, cum_ref):
    tq, tk = ATTN_TQ, ATTN_TK
    n_diag = tq // tk
    hb = pl.program_id(1)
    i = pl.program_id(2)
    row = lax.broadcasted_iota(jnp.int32, (tk, tq), 0)
    col = lax.broadcasted_iota(jnp.int32, (tk, tq), 1)
    triu = (lax.broadcasted_iota(jnp.int32, (tk, tk), 1)
            >= lax.broadcasted_iota(jnp.int32, (tk, tk), 0)).astype(BF16)
    heads = range(HEADS_PER_BLOCK)
    pair_w = 2 * HEAD_DIM
    pair_row = lax.broadcasted_iota(jnp.int32, (pair_w, tq), 0) // HEAD_DIM
    qts = []
    for h in heads:
        pair = qt_ref[0, (h // 2) * pair_w:(h // 2 + 1) * pair_w, :]
        qts.append(jnp.where(pair_row == h % 2, pair, jnp.zeros_like(pair)))
    biases = [bias_ref[hb * HEADS_PER_BLOCK + h] for h in heads]

    def scores(start, key_offset):
        zs = [jnp.dot(k_ref[0, pl.ds(start, tk), (h // 2) * pair_w:(h // 2 + 1) * pair_w], qts[h],
                      preferred_element_type=F32) + biases[h] for h in heads]
        if key_offset is not None:
            zs = [jnp.where(row + key_offset < col, z, MASKED_SCORE) for z in zs]
        return zs

    def park(zs):
        sps = [(jnp.maximum(z, 0.0) + jnp.log(1.0 + jnp.exp2(_neg_abs(z) * LOG2E))).astype(BF16) for z in zs]
        cums = [jnp.dot(triu, sp, preferred_element_type=F32) for sp in sps]
        for h in heads:
            z_ref[h] = zs[h]
            cum_ref[h] = cums[h]

    def finish(start, right_sums):
        avs = [jnp.exp(z_ref[h] - cum_ref[h]).astype(BF16) for h in heads]
        outs = [jnp.dot(vt_ref[0, h * HEAD_DIM:(h + 1) * HEAD_DIM, pl.ds(start, tk)], avs[h],
                        preferred_element_type=F32) for h in heads]
        new_sums = []
        for h in heads:
            acc_ref[h] += outs[h] * jnp.exp(-right_sums[h])
            new_sums.append(right_sums[h] + cum_ref[h, 0:1, :])
        return tuple(new_sums)

    acc_ref[...] = jnp.zeros_like(acc_ref)
    sums = tuple(jnp.zeros((1, tq), F32) for _ in heads)
    base = i * tq
    for d in range(n_diag):
        start = pl.multiple_of(base + (n_diag - 1 - d) * tk, tk)
        zs = scores(start, (n_diag - 1 - d) * tk)
        if d > 0:
            sums = finish(pl.multiple_of(start + tk, tk), sums)
        park(zs)

    def body(step, sums):
        start = pl.multiple_of(base - (step + 1) * tk, tk)
        sums = finish(pl.multiple_of(start + tk, tk), sums)
        park(scores(start, None))
        return sums

    sums = lax.fori_loop(0, i * n_diag, body, sums)
    finish(0, sums)
    for h in heads:
        o_ref[0, :, h * HEAD_DIM:(h + 1) * HEAD_DIM] = acc_ref[h].T.astype(o_ref.dtype)


def _attn_call(qtb, kb, vtb, bias):
    b, t_len, d = kb.shape
    tq, tk = ATTN_TQ, ATTN_TK
    w = HEADS_PER_BLOCK * HEAD_DIM
    return pl.pallas_call(
        _attn_kernel,
        grid=(b, d // w, t_len // tq),
        in_specs=[pl.BlockSpec(memory_space=pltpu.SMEM),
                  pl.BlockSpec((1, w, tq), lambda bi, hb, i: (bi, hb, i)),
                  pl.BlockSpec((1, t_len, w), lambda bi, hb, i: (bi, 0, hb)),
                  pl.BlockSpec((1, w, t_len), lambda bi, hb, i: (bi, hb, 0))],
        out_specs=pl.BlockSpec((1, tq, w), lambda bi, hb, i: (bi, i, hb)),
        out_shape=jax.ShapeDtypeStruct((b, t_len, d), BF16),
        scratch_shapes=[pltpu.VMEM((HEADS_PER_BLOCK, HEAD_DIM, tq), F32),
                        pltpu.VMEM((HEADS_PER_BLOCK, tk, tq), F32), pltpu.VMEM((HEADS_PER_BLOCK, tk, tq), F32)],
        compiler_params=_params(3),
        name="sb_prompt",
    )(bias, qtb, kb, vtb)


def _dec_kernel(pt_ref, qbd_ref, bias_ref, knew_ref, vnew_ref, *rest, n_new):
    del pt_ref
    pages = DEC_PAGES_PER_STEP
    k_refs, v_refs = rest[:pages], rest[pages:2 * pages]
    o_ref, acc_ref, carry_ref = rest[2 * pages:]
    p = pl.program_id(1)
    gw = DEC_GROUP * HEAD_DIM
    gr = DEC_GROUP * n_new
    n_groups = N_HEADS // DEC_GROUP
    page = k_refs[0].shape[3]
    bias = bias_ref[...]
    r16 = lax.broadcasted_iota(jnp.int32, (gr, 1), 0)
    own = (lax.broadcasted_iota(jnp.int32, (gr, gw), 0) // n_new
           == lax.broadcasted_iota(jnp.int32, (gr, gw), 1) // HEAD_DIM)

    @pl.when(p == 0)
    def _():
        for g in range(n_groups):
            rs = slice(g * gr, (g + 1) * gr)
            cs = slice(g * gw, (g + 1) * gw)
            qg = qbd_ref[0, rs, :].astype(F32)
            zs, sps = [], []
            for j in range(n_new):
                z = jnp.sum(qg * knew_ref[0, j:j + 1, cs], axis=1, keepdims=True) + bias[rs]
                zs.append(z)
                sps.append(jnp.where(j < r16 % n_new, _softplus(z), 0.0))
            cum = jnp.zeros((gr, 1), F32)
            acc = jnp.zeros((gr, gw), F32)
            for j in reversed(range(n_new)):
                cum = cum + sps[j]
                a = jnp.where(j < r16 % n_new, jnp.exp(zs[j] - cum), 0.0)
                acc = acc + a * vnew_ref[0, j:j + 1, cs]
            acc_ref[rs, :] = acc
            carry_ref[rs, :] = cum

    jj = lax.broadcasted_iota(jnp.int32, (page, page), 0)
    ss = lax.broadcasted_iota(jnp.int32, (page, page), 1)
    tri = (jj >= ss).astype(BF16)
    groups = [slice(g * gw, (g + 1) * gw) for g in range(n_groups)]
    rows = [slice(g * gr, (g + 1) * gr) for g in range(n_groups)]
    zs = []
    for k_ref in k_refs:
        kp = k_ref[0].reshape(N_HEADS * HEAD_DIM, page)
        zs.append(jnp.concatenate(
            [jnp.dot(qbd_ref[0, rows[g], :], kp[groups[g]].astype(BF16), preferred_element_type=F32)
             for g in range(n_groups)], axis=0) + bias)
    cums = [jnp.dot(_softplus(z).astype(BF16), tri, preferred_element_type=F32) for z in zs]
    avs = [jnp.exp(z - cum).astype(BF16) for z, cum in zip(zs, cums)]
    outs = []
    for a, v_ref in zip(avs, v_refs):
        vp = v_ref[0].reshape(N_HEADS * HEAD_DIM, page)
        outs.append([_nt_dot(a[rows[g]], vp[groups[g]].astype(BF16)) for g in range(n_groups)])
    carry = carry_ref[...]
    for out, cum in zip(outs, cums):
        scale = jnp.exp(-carry)
        for g in range(n_groups):
            acc_ref[rows[g], :] += out[g] * scale[rows[g]]
        carry = carry + cum[:, :1]
    carry_ref[...] = carry

    @pl.when(p == pl.num_programs(1) - 1)
    def _():
        for g in range(n_groups):
            mine = jnp.where(own, acc_ref[g * gr:(g + 1) * gr, :], 0.0)
            o = mine[0:n_new]
            for h4 in range(1, DEC_GROUP):
                o = o + mine[h4 * n_new:(h4 + 1) * n_new]
            o_ref[0, :, g * gw:(g + 1) * gw] = o.astype(o_ref.dtype)


def _dec_call(page_table, qbd, bias_col, k_new, v_new, cache_kt, cache_vt):
    db, nr, gw = qbd.shape
    n_new, d = k_new.shape[1:]
    n_pages = page_table.shape[1]
    pages = DEC_PAGES_PER_STEP
    pt = page_table.reshape(-1)
    seq_spec = lambda rows, cols: pl.BlockSpec((1, rows, cols), lambda b, p, pt_ref: (b, 0, 0))

    def page_spec(k):
        return pl.BlockSpec((1,) + cache_kt.shape[1:],
                            lambda b, p, pt_ref: (pt_ref[b * n_pages + n_pages - 1 - p * pages - k], 0, 0, 0))

    grid_spec = pltpu.PrefetchScalarGridSpec(
        num_scalar_prefetch=1,
        grid=(db, n_pages // pages),
        in_specs=[seq_spec(nr, gw), pl.BlockSpec((nr, 1), lambda b, p, pt_ref: (0, 0)),
                  seq_spec(n_new, d), seq_spec(n_new, d)]
        + [page_spec(k) for k in range(pages)] * 2,
        out_specs=seq_spec(n_new, d),
        scratch_shapes=[pltpu.VMEM((nr, gw), F32), pltpu.VMEM((nr, 1), F32)],
    )
    return pl.pallas_call(
        functools.partial(_dec_kernel, n_new=n_new),
        grid_spec=grid_spec,
        out_shape=jax.ShapeDtypeStruct((db, n_new, d), BF16),
        compiler_params=_params(2),
        name="sb_decode",
    )(pt, qbd, bias_col, k_new, v_new, *([cache_kt] * pages), *([cache_vt] * pages))


def kernel(x_prompt, x_sample, cache_k, cache_v, page_table, norm_mix_g, a_w_in, a_v_g, a_ws, a_bs, a_w_out,
           kv_norm_g, w_k, w_v, b_w_q, b_sb_bias, b_w_o, norm_ffn_g, w_gate, w_up, w_down, norm_final_g):
    bsz, seq, d = x_prompt.shape
    db, n_l, _ = x_sample.shape
    depth = norm_mix_g.shape[0]
    n_a = a_w_in.shape[0]
    n_pages = page_table.shape[1]
    assert depth == 2 and n_a == 1 and b_w_q.shape[0] == 1, "one SGU layer followed by one stick-breaking layer"
    assert d == N_HEADS * HEAD_DIM and SGU_CHUNK % n_l == 0 and n_pages % DEC_PAGES_PER_STEP == 0

    row = lambda g: g.reshape(1, d)
    bf = lambda w: w.astype(BF16)

    gd = d // SGU_GROUPS
    mix_p = a_ws[0]
    bsb_p = jnp.repeat(a_bs[0].T, gd, axis=1)
    reps = SGU_CHUNK // n_l
    eye = jnp.eye(reps, dtype=F32)
    mix_s = jnp.einsum('ab,gts->gatbs', eye, a_ws[0][:, :n_l, :n_l]).reshape(SGU_GROUPS, SGU_CHUNK, SGU_CHUNK)
    bsb_s = jnp.repeat(jnp.tile(a_bs[0][:, :n_l], (1, reps)).T, gd, axis=1)

    w_in, w_out = bf(a_w_in[0]), bf(a_w_out[0])
    xp = x_prompt.reshape(bsz * seq, d)
    xs = x_sample.reshape(db * n_l, d)
    (hp,) = _sgu_call(xp, row(norm_mix_g[0]), w_in, row(a_v_g[0]), mix_p, bsb_p, w_out, emit_v=False)
    hs, v_rows = _sgu_call(xs, row(norm_mix_g[0]), w_in, row(a_v_g[0]), mix_s, bsb_s, w_out, emit_v=True)
    wg0, wu0, wd0 = bf(w_gate[0]), bf(w_up[0]), bf(w_down[0])
    hp = _ffn_call(hp, None, None, row(norm_ffn_g[0]), wg0, wu0, wd0, None)
    hs = _ffn_call(hs, None, None, row(norm_ffn_g[0]), wg0, wu0, wd0, None)

    wq, wo = bf(b_w_q[0]), bf(b_w_o[0])
    kt_p, vt_p, kb_p, vtb_p, qtb_p = _kvq_cols_call(hp.reshape(bsz, seq, d), row(kv_norm_g), row(norm_mix_g[1]),
                                                    bf(w_k), bf(w_k.T), bf(w_v.T), bf(b_w_q[0].T))
    k_s, v_s, q_s = _kvq_rows_call(hs, row(kv_norm_g), row(norm_mix_g[1]), bf(w_k), bf(w_v), wq)

    bias = b_sb_bias[0].astype(F32)
    o_p = _attn_call(qtb_p, kb_p, vtb_p, bias)

    n_groups = N_HEADS // DEC_GROUP
    q5 = q_s.reshape(db, n_l, n_groups, DEC_GROUP, HEAD_DIM)
    qbd = jnp.einsum('blghd,hk->bghlkd', q5, jnp.eye(DEC_GROUP, dtype=F32))
    qbd = qbd.reshape(db, N_HEADS * n_l, DEC_GROUP * HEAD_DIM).astype(BF16)
    bias_col = jnp.repeat(bias, n_l).reshape(N_HEADS * n_l, 1)
    cache_kt = jnp.transpose(cache_k, (0, 2, 3, 1))
    cache_vt = jnp.transpose(cache_v, (0, 2, 3, 1))
    o_s = _dec_call(page_table, qbd, bias_col, k_s.reshape(db, n_l, d), v_s.reshape(db, n_l, d),
                    cache_kt, cache_vt)

    wg1, wu1, wd1 = bf(w_gate[1]), bf(w_up[1]), bf(w_down[1])
    y_p = _ffn_call(hp, o_p.reshape(bsz * seq, d), wo, row(norm_ffn_g[1]), wg1, wu1, wd1, row(norm_final_g))
    y_s = _ffn_call(hs, o_s.reshape(db * n_l, d), wo, row(norm_ffn_g[1]), wg1, wu1, wd1, row(norm_final_g))

    heads = (N_HEADS, HEAD_DIM)
    to_bthd = lambda xt: jnp.transpose(xt.reshape(bsz, *heads, seq), (0, 3, 1, 2))
    return (y_p.reshape(bsz, seq, d), y_s.reshape(db, n_l, d),
            to_bthd(kt_p), to_bthd(vt_p),
            k_s.reshape(db, n_l, *heads), v_s.reshape(db, n_l, *heads),
            v_rows.reshape(n_a, db, n_l, d))
```

```python
import functools
import math

import jax
import jax.numpy as jnp
from jax import lax
from jax.experimental import pallas as pl
from jax.experimental.pallas import tpu as pltpu

F32 = jnp.float32
BF16 = jnp.bfloat16

NORM_EPS = 1e-6
LOG2E = math.log2(math.e)
SGU_CHUNK = 128
SGU_GROUPS = 8
N_HEADS = 16
HEAD_DIM = 64
HEADS_PER_BLOCK = 4
DEC_GROUP = 4
V7X_VMEM_LIMIT = 56 * 1024 * 1024

ROW_TILE = 256
ATTN_TQ = 512
ATTN_TK = 256
MASKED_SCORE = -1e30
DEC_PAGES_PER_STEP = 8


def _const_spec(shape):
    nd = len(shape)
    return pl.BlockSpec(shape, lambda *_: (0,) * nd, pipeline_mode=pl.Buffered(1))


def _params(n_axes):
    return pltpu.CompilerParams(dimension_semantics=("arbitrary",) * n_axes, vmem_limit_bytes=V7X_VMEM_LIMIT)


def _rms_scale(x):
    return x * lax.rsqrt(jnp.mean(x * x, axis=-1, keepdims=True) + NORM_EPS)


def _gelu_exact(x):
    return 0.5 * x * (1.0 + lax.erf(x * math.sqrt(0.5)))


def _softplus(z):
    return jnp.maximum(z, 0.0) + jnp.log(1.0 + jnp.exp(-jnp.abs(z)))


def _nt_dot(a, b):
    return lax.dot_general(a, b, (((1,), (1,)), ((), ())), preferred_element_type=F32)


def _sgu_kernel(x_ref, gmix_ref, win_ref, vg_ref, mix_ref, bsb_ref, wout_ref, *rest, emit_v):
    if emit_v:
        h_ref, v_ref, gated_ref = rest
    else:
        h_ref, gated_ref = rest
    d = x_ref.shape[1]
    x = x_ref[...]
    xn = (_rms_scale(x) * gmix_ref[...]).astype(BF16)
    z = _gelu_exact(jnp.dot(xn, win_ref[...], preferred_element_type=F32))
    u = z[:, :d]
    vn = _rms_scale(z[:, d:]) * vg_ref[...]
    if emit_v:
        v_ref[...] = vn
    vb = vn.astype(BF16)
    row = lax.broadcasted_iota(jnp.int32, (SGU_CHUNK, SGU_CHUNK), 0)
    col = lax.broadcasted_iota(jnp.int32, (SGU_CHUNK, SGU_CHUNK), 1)
    tril = row >= col
    gd = d // SGU_GROUPS
    for g in range(SGU_GROUPS):
        w = jnp.where(tril, mix_ref[g], 0.0).astype(BF16)
        cs = slice(g * gd, (g + 1) * gd)
        for c in range(x.shape[0] // SGU_CHUNK):
            rs = slice(c * SGU_CHUNK, (c + 1) * SGU_CHUNK)
            s = jnp.dot(w, vb[rs, cs], preferred_element_type=F32) + bsb_ref[:, cs]
            gated_ref[rs, cs] = (u[rs, cs] * s).astype(BF16)
    h_ref[...] = x + jnp.dot(gated_ref[...], wout_ref[...], preferred_element_type=F32)


def _sgu_call(x, gmix, w_in, v_g, mix, bsb, w_out, *, emit_v):
    n, d = x.shape
    tm = ROW_TILE
    row_spec = pl.BlockSpec((tm, d), lambda i: (i, 0))
    out_shape = [jax.ShapeDtypeStruct((n, d), F32)]
    out_specs = [row_spec]
    if emit_v:
        out_shape.append(jax.ShapeDtypeStruct((n, d), F32))
        out_specs.append(row_spec)
    return pl.pallas_call(
        functools.partial(_sgu_kernel, emit_v=emit_v),
        grid=(n // tm,),
        in_specs=[row_spec, _const_spec((1, d)), _const_spec(w_in.shape), _const_spec((1, d)),
                  _const_spec(mix.shape), _const_spec(bsb.shape), _const_spec(w_out.shape)],
        out_specs=out_specs,
        out_shape=out_shape,
        scratch_shapes=[pltpu.VMEM((tm, d), BF16)],
        compiler_params=_params(1),
        name="sgu_emit_v" if emit_v else "sgu",
    )(x, gmix, w_in, v_g, mix, bsb, w_out)


def _ffn_kernel(*refs, with_proj, final_norm):
    refs = list(refs)
    h_ref = refs.pop(0)
    if with_proj:
        o_ref = refs.pop(0)
        wo_ref = refs.pop(0)
    g_ref, wg_ref, wu_ref, wd_ref = refs[:4]
    refs = refs[4:]
    if final_norm:
        gf_ref = refs.pop(0)
    (out_ref,) = refs
    h = h_ref[...]
    if with_proj:
        h = h + jnp.dot(o_ref[...], wo_ref[...], preferred_element_type=F32)
    xn = (_rms_scale(h) * g_ref[...]).astype(BF16)
    gate = jnp.dot(xn, wg_ref[...], preferred_element_type=F32)
    up = jnp.dot(xn, wu_ref[...], preferred_element_type=F32)
    act = (gate * jax.nn.sigmoid(gate) * up).astype(BF16)
    h = h + jnp.dot(act, wd_ref[...], preferred_element_type=F32)
    if final_norm:
        h = _rms_scale(h) * gf_ref[...]
    out_ref[...] = h


def _ffn_call(h, o, w_o, g, w_gate, w_up, w_down, g_final):
    n, d = h.shape
    tm = ROW_TILE
    with_proj = o is not None
    final_norm = g_final is not None
    row_spec = pl.BlockSpec((tm, d), lambda i: (i, 0))
    args, in_specs = [h], [row_spec]
    if with_proj:
        args += [o, w_o]
        in_specs += [row_spec, _const_spec(w_o.shape)]
    args += [g, w_gate, w_up, w_down]
    in_specs += [_const_spec((1, d)), _const_spec(w_gate.shape), _const_spec(w_up.shape),
                 _const_spec(w_down.shape)]
    if final_norm:
        args.append(g_final)
        in_specs.append(_const_spec((1, d)))
    return pl.pallas_call(
        functools.partial(_ffn_kernel, with_proj=with_proj, final_norm=final_norm),
        grid=(n // tm,),
        in_specs=in_specs,
        out_specs=row_spec,
        out_shape=jax.ShapeDtypeStruct((n, d), F32),
        compiler_params=_params(1),
        name="ffn_proj_final" if with_proj else "ffn",
    )(*args)


def _kvq_rows_kernel(h_ref, gkv_ref, gq_ref, wk_ref, wv_ref, wq_ref, k_ref, v_ref, q_ref):
    xr = _rms_scale(h_ref[...])
    hkv = (xr * gkv_ref[...]).astype(BF16)
    hq = (xr * gq_ref[...]).astype(BF16)
    k_ref[...] = jnp.dot(hkv, wk_ref[...], preferred_element_type=F32)
    v_ref[...] = jnp.dot(hkv, wv_ref[...], preferred_element_type=F32)
    q_ref[...] = jnp.dot(hq, wq_ref[...], preferred_element_type=F32) * (HEAD_DIM ** -0.5)


def _kvq_rows_call(h, g_kv, g_q, w_k, w_v, w_q):
    n, d = h.shape
    tm = ROW_TILE
    row_spec = pl.BlockSpec((tm, d), lambda i: (i, 0))
    wspec = _const_spec(w_k.shape)
    gspec = _const_spec((1, d))
    return pl.pallas_call(
        _kvq_rows_kernel,
        grid=(n // tm,),
        in_specs=[row_spec, gspec, gspec, wspec, wspec, wspec],
        out_specs=[row_spec] * 3,
        out_shape=[jax.ShapeDtypeStruct((n, d), F32)] * 3,
        compiler_params=_params(1),
        name="kvq_rows",
    )(h, g_kv, g_q, w_k, w_v, w_q)


def _kvq_cols_kernel(h_ref, gkv_ref, gq_ref, wk_ref, wkt_ref, wvt_ref, wqt_ref,
                     kt_ref, vt_ref, kb_ref, vtb_ref, qtb_ref):
    xr = _rms_scale(h_ref[0])
    hkv = (xr * gkv_ref[...]).astype(BF16)
    hq = (xr * gq_ref[...]).astype(BF16)
    kt_ref[0] = _nt_dot(wkt_ref[...], hkv)
    vt = _nt_dot(wvt_ref[...], hkv)
    vt_ref[0] = vt
    vtb_ref[0] = vt.astype(BF16)
    kb_ref[0] = jnp.dot(hkv, wk_ref[...], preferred_element_type=F32).astype(BF16)
    qtb_ref[0] = (_nt_dot(wqt_ref[...], hq) * (HEAD_DIM ** -0.5)).astype(BF16)


def _kvq_cols_call(h, g_kv, g_q, w_k, w_kt, w_vt, w_qt):
    b, t_len, d = h.shape
    tm = ROW_TILE
    row_spec = pl.BlockSpec((1, tm, d), lambda bi, i: (bi, i, 0))
    col_spec = pl.BlockSpec((1, d, tm), lambda bi, i: (bi, 0, i))
    wspec = _const_spec(w_k.shape)
    gspec = _const_spec((1, d))
    cols = lambda dt: jax.ShapeDtypeStruct((b, d, t_len), dt)
    return pl.pallas_call(
        _kvq_cols_kernel,
        grid=(b, t_len // tm),
        in_specs=[row_spec, gspec, gspec, wspec, wspec, wspec, wspec],
        out_specs=[col_spec, col_spec, row_spec, col_spec, col_spec],
        out_shape=[cols(F32), cols(F32), jax.ShapeDtypeStruct((b, t_len, d), BF16), cols(BF16), cols(BF16)],
        compiler_params=_params(2),
        name="kvq_cols",
    )(h, g_kv, g_q, w_k, w_kt, w_vt, w_qt)


def _attn_kernel(bias_ref, qt_ref, k_ref, vt_ref, o_ref, acc_ref, z_ref, cum_ref):
    tq, tk = ATTN_TQ, ATTN_TK
    hb = pl.program_id(1)
    i = pl.program_id(2)
    row = lax.broadcasted_iota(jnp.int32, (tk, tq), 0)
    col = lax.broadcasted_iota(jnp.int32, (tk, tq), 1)
    triu = (lax.broadcasted_iota(jnp.int32, (tk, tk), 1)
            >= lax.broadcasted_iota(jnp.int32, (tk, tk), 0)).astype(BF16)
    heads = range(HEADS_PER_BLOCK)
    pair_w = 2 * HEAD_DIM
    pair_row = lax.broadcasted_iota(jnp.int32, (pair_w, tq), 0)
    ones = jnp.ones((tk, pair_w), BF16)
    qbs = []
    for h in heads:
        pair = qt_ref[0, (h // 2) * pair_w:(h // 2 + 1) * pair_w, :]
        qt = jnp.where(pair_row // HEAD_DIM == h % 2, pair, jnp.zeros_like(pair))
        bias = bias_ref[hb * HEADS_PER_BLOCK + h]
        bias_hi = bias.astype(BF16).astype(F32)
        bias_rows = jnp.where(pair_row == 0, bias_hi, jnp.where(pair_row == 1, bias - bias_hi, 0.0))
        qbs.append(jnp.concatenate([qt, bias_rows.astype(BF16)], axis=0))

    def park(start, key_offset, slot):
        for h in heads:
            k2 = k_ref[0, pl.ds(start, tk), (h // 2) * pair_w:(h // 2 + 1) * pair_w]
            z = jnp.dot(jnp.concatenate([k2, ones], axis=1), qbs[h], preferred_element_type=F32)
            if key_offset is not None:
                z = jnp.where(row + key_offset < col, z, MASKED_SCORE)
            z_ref[slot, h] = z
        sps = [(jnp.maximum(z_ref[slot, h], 0.0)
                + jnp.log(1.0 + jnp.exp2(jnp.abs(z_ref[slot, h]) * (-LOG2E)))).astype(BF16) for h in heads]
        for h in heads:
            cum_ref[slot, h] = jnp.dot(triu, sps[h], preferred_element_type=F32)

    def finish(start, slot, right_sums):
        avs = [jnp.exp(z_ref[slot, h] - cum_ref[slot, h]).astype(BF16) for h in heads]
        outs = [jnp.dot(vt_ref[0, h * HEAD_DIM:(h + 1) * HEAD_DIM, pl.ds(start, tk)], avs[h],
                        preferred_element_type=F32) for h in heads]
        new_sums = []
        for h in heads:
            acc_ref[h] += outs[h] * jnp.exp(-right_sums[h])
            new_sums.append(right_sums[h] + cum_ref[slot, h, 0:1, :])
        return tuple(new_sums)

    acc_ref[...] = jnp.zeros_like(acc_ref)
    sums = tuple(jnp.zeros((1, tq), F32) for _ in heads)
    base = i * tq
    park(pl.multiple_of(base + tk, tk), tk, 0)
    park(pl.multiple_of(base, tk), 0, 1)
    sums = finish(pl.multiple_of(base + tk, tk), 0, sums)

    def body(j, sums):
        start = pl.multiple_of(base - (2 * j + 1) * tk, tk)
        park(start, None, 0)
        sums = finish(pl.multiple_of(start + tk, tk), 1, sums)
        park(pl.multiple_of(start - tk, tk), None, 1)
        return finish(start, 0, sums)

    sums = lax.fori_loop(0, i, body, sums)
    finish(0, 1, sums)
    for h in heads:
        o_ref[0, :, h * HEAD_DIM:(h + 1) * HEAD_DIM] = acc_ref[h].T.astype(o_ref.dtype)


def _attn_call(qtb, kb, vtb, bias):
    b, t_len, d = kb.shape
    tq, tk = ATTN_TQ, ATTN_TK
    assert tq == 2 * tk, "the tile pipeline alternates two buffer slots per query tile"
    w = HEADS_PER_BLOCK * HEAD_DIM
    park_buffer = pltpu.VMEM((2, HEADS_PER_BLOCK, tk, tq), F32)
    return pl.pallas_call(
        _attn_kernel,
        grid=(b, d // w, t_len // tq),
        in_specs=[pl.BlockSpec(memory_space=pltpu.SMEM),
                  pl.BlockSpec((1, w, tq), lambda bi, hb, i: (bi, hb, i)),
                  pl.BlockSpec((1, t_len, w), lambda bi, hb, i: (bi, 0, hb)),
                  pl.BlockSpec((1, w, t_len), lambda bi, hb, i: (bi, hb, 0))],
        out_specs=pl.BlockSpec((1, tq, w), lambda bi, hb, i: (bi, i, hb)),
        out_shape=jax.ShapeDtypeStruct((b, t_len, d), BF16),
        scratch_shapes=[pltpu.VMEM((HEADS_PER_BLOCK, HEAD_DIM, tq), F32), park_buffer, park_buffer],
        compiler_params=_params(3),
        name="sb_prompt",
    )(bias, qtb, kb, vtb)


def _dec_kernel(pt_ref, qbd_ref, bias_ref, knew_ref, vnew_ref, *rest, n_new):
    del pt_ref
    pages = DEC_PAGES_PER_STEP
    k_refs, v_refs = rest[:pages], rest[pages:2 * pages]
    o_ref, acc_ref, carry_ref = rest[2 * pages:]
    p = pl.program_id(1)
    gw = DEC_GROUP * HEAD_DIM
    gr = DEC_GROUP * n_new
    n_groups = N_HEADS // DEC_GROUP
    page = k_refs[0].shape[3]
    bias = bias_ref[...]
    r16 = lax.broadcasted_iota(jnp.int32, (gr, 1), 0)
    own = (lax.broadcasted_iota(jnp.int32, (gr, gw), 0) // n_new
           == lax.broadcasted_iota(jnp.int32, (gr, gw), 1) // HEAD_DIM)

    @pl.when(p == 0)
    def _():
        for g in range(n_groups):
            rs = slice(g * gr, (g + 1) * gr)
            cs = slice(g * gw, (g + 1) * gw)
            qg = qbd_ref[0, rs, :].astype(F32)
            zs, sps = [], []
            for j in range(n_new):
                z = jnp.sum(qg * knew_ref[0, j:j + 1, cs], axis=1, keepdims=True) + bias[rs]
                zs.append(z)
                sps.append(jnp.where(j < r16 % n_new, _softplus(z), 0.0))
            cum = jnp.zeros((gr, 1), F32)
            acc = jnp.zeros((gr, gw), F32)
            for j in reversed(range(n_new)):
                cum = cum + sps[j]
                a = jnp.where(j < r16 % n_new, jnp.exp(zs[j] - cum), 0.0)
                acc = acc + a * vnew_ref[0, j:j + 1, cs]
            acc_ref[rs, :] = acc
            carry_ref[rs, :] = cum

    jj = lax.broadcasted_iota(jnp.int32, (page, page), 0)
    ss = lax.broadcasted_iota(jnp.int32, (page, page), 1)
    tri = (jj >= ss).astype(BF16)
    groups = [slice(g * gw, (g + 1) * gw) for g in range(n_groups)]
    rows = [slice(g * gr, (g + 1) * gr) for g in range(n_groups)]
    zs = []
    for k_ref in k_refs:
        kp = k_ref[0].reshape(N_HEADS * HEAD_DIM, page)
        zs.append(jnp.concatenate(
            [jnp.dot(qbd_ref[0, rows[g], :], kp[groups[g]].astype(BF16), preferred_element_type=F32)
             for g in range(n_groups)], axis=0) + bias)
    cums = [jnp.dot(_softplus(z).astype(BF16), tri, preferred_element_type=F32) for z in zs]
    avs = [jnp.exp(z - cum).astype(BF16) for z, cum in zip(zs, cums)]
    outs = []
    for a, v_ref in zip(avs, v_refs):
        vp = v_ref[0].reshape(N_HEADS * HEAD_DIM, page)
        outs.append([_nt_dot(a[rows[g]], vp[groups[g]].astype(BF16)) for g in range(n_groups)])
    carry = carry_ref[...]
    for out, cum in zip(outs, cums):
        scale = jnp.exp(-carry)
        for g in range(n_groups):
            acc_ref[rows[g], :] += out[g] * scale[rows[g]]
        carry = carry + cum[:, :1]
    carry_ref[...] = carry

    @pl.when(p == pl.num_programs(1) - 1)
    def _():
        for g in range(n_groups):
            mine = jnp.where(own, acc_ref[g * gr:(g + 1) * gr, :], 0.0)
            o = mine[0:n_new]
            for h4 in range(1, DEC_GROUP):
                o = o + mine[h4 * n_new:(h4 + 1) * n_new]
            o_ref[0, :, g * gw:(g + 1) * gw] = o.astype(o_ref.dtype)


def _dec_call(page_table, qbd, bias_col, k_new, v_new, cache_kt, cache_vt):
    db, nr, gw = qbd.shape
    n_new, d = k_new.shape[1:]
    n_pages = page_table.shape[1]
    pages = DEC_PAGES_PER_STEP
    pt = page_table.reshape(-1)
    seq_spec = lambda rows, cols: pl.BlockSpec((1, rows, cols), lambda b, p, pt_ref: (b, 0, 0))

    def page_spec(k):
        return pl.BlockSpec((1,) + cache_kt.shape[1:],
                            lambda b, p, pt_ref: (pt_ref[b * n_pages + n_pages - 1 - p * pages - k], 0, 0, 0))

    grid_spec = pltpu.PrefetchScalarGridSpec(
        num_scalar_prefetch=1,
        grid=(db, n_pages // pages),
        in_specs=[seq_spec(nr, gw), pl.BlockSpec((nr, 1), lambda b, p, pt_ref: (0, 0)),
                  seq_spec(n_new, d), seq_spec(n_new, d)]
        + [page_spec(k) for k in range(pages)] * 2,
        out_specs=seq_spec(n_new, d),
        scratch_shapes=[pltpu.VMEM((nr, gw), F32), pltpu.VMEM((nr, 1), F32)],
    )
    return pl.pallas_call(
        functools.partial(_dec_kernel, n_new=n_new),
        grid_spec=grid_spec,
        out_shape=jax.ShapeDtypeStruct((db, n_new, d), BF16),
        compiler_params=_params(2),
        name="sb_decode",
    )(pt, qbd, bias_col, k_new, v_new, *([cache_kt] * pages), *([cache_vt] * pages))


def kernel(x_prompt, x_sample, cache_k, cache_v, page_table, norm_mix_g, a_w_in, a_v_g, a_ws, a_bs, a_w_out,
           kv_norm_g, w_k, w_v, b_w_q, b_sb_bias, b_w_o, norm_ffn_g, w_gate, w_up, w_down, norm_final_g):
    bsz, seq, d = x_prompt.shape
    db, n_l, _ = x_sample.shape
    depth = norm_mix_g.shape[0]
    n_a = a_w_in.shape[0]
    n_pages = page_table.shape[1]
    assert depth == 2 and n_a == 1 and b_w_q.shape[0] == 1, "one SGU layer followed by one stick-breaking layer"
    assert d == N_HEADS * HEAD_DIM and SGU_CHUNK % n_l == 0 and n_pages % DEC_PAGES_PER_STEP == 0

    row = lambda g: g.reshape(1, d)
    bf = lambda w: w.astype(BF16)

    gd = d // SGU_GROUPS
    mix_p = a_ws[0]
    bsb_p = jnp.repeat(a_bs[0].T, gd, axis=1)
    reps = SGU_CHUNK // n_l
    eye = jnp.eye(reps, dtype=F32)
    mix_s = jnp.einsum('ab,gts->gatbs', eye, a_ws[0][:, :n_l, :n_l]).reshape(SGU_GROUPS, SGU_CHUNK, SGU_CHUNK)
    bsb_s = jnp.repeat(jnp.tile(a_bs[0][:, :n_l], (1, reps)).T, gd, axis=1)

    w_in, w_out = bf(a_w_in[0]), bf(a_w_out[0])
    xp = x_prompt.reshape(bsz * seq, d)
    xs = x_sample.reshape(db * n_l, d)
    (hp,) = _sgu_call(xp, row(norm_mix_g[0]), w_in, row(a_v_g[0]), mix_p, bsb_p, w_out, emit_v=False)
    hs, v_rows = _sgu_call(xs, row(norm_mix_g[0]), w_in, row(a_v_g[0]), mix_s, bsb_s, w_out, emit_v=True)
    wg0, wu0, wd0 = bf(w_gate[0]), bf(w_up[0]), bf(w_down[0])
    hp = _ffn_call(hp, None, None, row(norm_ffn_g[0]), wg0, wu0, wd0, None)
    hs = _ffn_call(hs, None, None, row(norm_ffn_g[0]), wg0, wu0, wd0, None)

    wq, wo = bf(b_w_q[0]), bf(b_w_o[0])
    kt_p, vt_p, kb_p, vtb_p, qtb_p = _kvq_cols_call(hp.reshape(bsz, seq, d), row(kv_norm_g), row(norm_mix_g[1]),
                                                    bf(w_k), bf(w_k.T), bf(w_v.T), bf(b_w_q[0].T))
    k_s, v_s, q_s = _kvq_rows_call(hs, row(kv_norm_g), row(norm_mix_g[1]), bf(w_k), bf(w_v), wq)

    bias = b_sb_bias[0].astype(F32)
    o_p = _attn_call(qtb_p, kb_p, vtb_p, bias)

    n_groups = N_HEADS // DEC_GROUP
    q5 = q_s.reshape(db, n_l, n_groups, DEC_GROUP, HEAD_DIM)
    qbd = jnp.einsum('blghd,hk->bghlkd', q5, jnp.eye(DEC_GROUP, dtype=F32))
    qbd = qbd.reshape(db, N_HEADS * n_l, DEC_GROUP * HEAD_DIM).astype(BF16)
    bias_col = jnp.repeat(bias, n_l).reshape(N_HEADS * n_l, 1)
    cache_kt = jnp.transpose(cache_k, (0, 2, 3, 1))
    cache_vt = jnp.transpose(cache_v, (0, 2, 3, 1))
    o_s = _dec_call(page_table, qbd, bias_col, k_s.reshape(db, n_l, d), v_s.reshape(db, n_l, d),
                    cache_kt, cache_vt)

    wg1, wu1, wd1 = bf(w_gate[1]), bf(w_up[1]), bf(w_down[1])
    y_p = _ffn_call(hp, o_p.reshape(bsz * seq, d), wo, row(norm_ffn_g[1]), wg1, wu1, wd1, row(norm_final_g))
    y_s = _ffn_call(hs, o_s.reshape(db * n_l, d), wo, row(norm_ffn_g[1]), wg1, wu1, wd1, row(norm_final_g))

    heads = (N_HEADS, HEAD_DIM)
    to_bthd = lambda xt: jnp.transpose(xt.reshape(bsz, *heads, seq), (0, 3, 1, 2))
    return (y_p.reshape(bsz, seq, d), y_s.reshape(db, n_l, d),
            to_bthd(kt_p), to_bthd(vt_p),
            k_s.reshape(db, n_l, *heads), v_s.reshape(db, n_l, *heads),
            v_rows.reshape(n_a, db, n_l, d))
```

```python
import functools
import math

import jax
import jax.numpy as jnp
from jax import lax
from jax.experimental import pallas as pl
from jax.experimental.pallas import tpu as pltpu

F32 = jnp.float32
BF16 = jnp.bfloat16

NORM_EPS = 1e-6
LOG2E = math.log2(math.e)
SGU_CHUNK = 128
SGU_GROUPS = 8
N_HEADS = 16
HEAD_DIM = 64
HEADS_PER_BLOCK = 4
DEC_GROUP = 4
V7X_VMEM_LIMIT = 56 * 1024 * 1024

ROW_TILE = 512
ATTN_TQ = 512
ATTN_TK = 256
MASKED_SCORE = -1e30
DEC_PAGES_PER_STEP = 8


def _const_spec(shape):
    nd = len(shape)
    return pl.BlockSpec(shape, lambda *_: (0,) * nd, pipeline_mode=pl.Buffered(1))


def _params(n_axes):
    return pltpu.CompilerParams(dimension_semantics=("arbitrary",) * n_axes, vmem_limit_bytes=V7X_VMEM_LIMIT)


def _rms_scale(x):
    return x * lax.rsqrt(jnp.mean(x * x, axis=-1, keepdims=True) + NORM_EPS)


def _gelu_exact(x):
    return 0.5 * x * (1.0 + lax.erf(x * math.sqrt(0.5)))


def _softplus(z):
    return jnp.maximum(z, 0.0) + jnp.log(1.0 + jnp.exp(-jnp.abs(z)))


def _nt_dot(a, b):
    return lax.dot_general(a, b, (((1,), (1,)), ((), ())), preferred_element_type=F32)


def _sgu_kernel(x_ref, gmix_ref, win_ref, vg_ref, mix_ref, bsb_ref, wout_ref, *rest, emit_v):
    if emit_v:
        h_ref, v_ref, gated_ref = rest
    else:
        h_ref, gated_ref = rest
    d = x_ref.shape[1]
    x = x_ref[...]
    xn = (_rms_scale(x) * gmix_ref[...]).astype(BF16)
    z = _gelu_exact(jnp.dot(xn, win_ref[...], preferred_element_type=F32))
    u = z[:, :d]
    vn = _rms_scale(z[:, d:]) * vg_ref[...]
    if emit_v:
        v_ref[...] = vn
    vb = vn.astype(BF16)
    row = lax.broadcasted_iota(jnp.int32, (SGU_CHUNK, SGU_CHUNK), 0)
    col = lax.broadcasted_iota(jnp.int32, (SGU_CHUNK, SGU_CHUNK), 1)
    tril = row >= col
    gd = d // SGU_GROUPS
    for g in range(SGU_GROUPS):
        w = jnp.where(tril, mix_ref[g], 0.0).astype(BF16)
        cs = slice(g * gd, (g + 1) * gd)
        for c in range(x.shape[0] // SGU_CHUNK):
            rs = slice(c * SGU_CHUNK, (c + 1) * SGU_CHUNK)
            s = jnp.dot(w, vb[rs, cs], preferred_element_type=F32) + bsb_ref[:, cs]
            gated_ref[rs, cs] = (u[rs, cs] * s).astype(BF16)
    h_ref[...] = x + jnp.dot(gated_ref[...], wout_ref[...], preferred_element_type=F32)


def _sgu_call(x, gmix, w_in, v_g, mix, bsb, w_out, *, emit_v):
    n, d = x.shape
    tm = ROW_TILE
    assert n % tm == 0
    row_spec = pl.BlockSpec((tm, d), lambda i: (i, 0))
    out_shape = [jax.ShapeDtypeStruct((n, d), F32)]
    out_specs = [row_spec]
    if emit_v:
        out_shape.append(jax.ShapeDtypeStruct((n, d), F32))
        out_specs.append(row_spec)
    return pl.pallas_call(
        functools.partial(_sgu_kernel, emit_v=emit_v),
        grid=(n // tm,),
        in_specs=[row_spec, _const_spec((1, d)), _const_spec(w_in.shape), _const_spec((1, d)),
                  _const_spec(mix.shape), _const_spec(bsb.shape), _const_spec(w_out.shape)],
        out_specs=out_specs,
        out_shape=out_shape,
        scratch_shapes=[pltpu.VMEM((tm, d), BF16)],
        compiler_params=_params(1),
        name="sgu_emit_v" if emit_v else "sgu",
    )(x, gmix, w_in, v_g, mix, bsb, w_out)


def _ffn_kernel(*refs, with_proj, final_norm):
    refs = list(refs)
    h_ref = refs.pop(0)
    if with_proj:
        o_ref = refs.pop(0)
        wo_ref = refs.pop(0)
    g_ref, wg_ref, wu_ref, wd_ref = refs[:4]
    refs = refs[4:]
    if final_norm:
        gf_ref = refs.pop(0)
    (out_ref,) = refs
    h = h_ref[...]
    if with_proj:
        h = h + jnp.dot(o_ref[...], wo_ref[...], preferred_element_type=F32)
    xn = (_rms_scale(h) * g_ref[...]).astype(BF16)
    gate = jnp.dot(xn, wg_ref[...], preferred_element_type=F32)
    up = jnp.dot(xn, wu_ref[...], preferred_element_type=F32)
    act = (gate * jax.nn.sigmoid(gate) * up).astype(BF16)
    h = h + jnp.dot(act, wd_ref[...], preferred_element_type=F32)
    if final_norm:
        h = _rms_scale(h) * gf_ref[...]
    out_ref[...] = h


def _ffn_call(h, o, w_o, g, w_gate, w_up, w_down, g_final):
    n, d = h.shape
    tm = ROW_TILE
    assert n % tm == 0
    with_proj = o is not None
    final_norm = g_final is not None
    row_spec = pl.BlockSpec((tm, d), lambda i: (i, 0))
    args, in_specs = [h], [row_spec]
    if with_proj:
        args += [o, w_o]
        in_specs += [row_spec, _const_spec(w_o.shape)]
    args += [g, w_gate, w_up, w_down]
    in_specs += [_const_spec((1, d)), _const_spec(w_gate.shape), _const_spec(w_up.shape),
                 _const_spec(w_down.shape)]
    if final_norm:
        args.append(g_final)
        in_specs.append(_const_spec((1, d)))
    return pl.pallas_call(
        functools.partial(_ffn_kernel, with_proj=with_proj, final_norm=final_norm),
        grid=(n // tm,),
        in_specs=in_specs,
        out_specs=row_spec,
        out_shape=jax.ShapeDtypeStruct((n, d), F32),
        compiler_params=_params(1),
        name="ffn_proj_final" if with_proj else "ffn",
    )(*args)


def _kvq_rows_kernel(h_ref, gkv_ref, gq_ref, wk_ref, wv_ref, wq_ref, k_ref, v_ref, q_ref):
    xr = _rms_scale(h_ref[...])
    hkv = (xr * gkv_ref[...]).astype(BF16)
    hq = (xr * gq_ref[...]).astype(BF16)
    k_ref[...] = jnp.dot(hkv, wk_ref[...], preferred_element_type=F32)
    v_ref[...] = jnp.dot(hkv, wv_ref[...], preferred_element_type=F32)
    q_ref[...] = jnp.dot(hq, wq_ref[...], preferred_element_type=F32) * (HEAD_DIM ** -0.5)


def _kvq_rows_call(h, g_kv, g_q, w_k, w_v, w_q):
    n, d = h.shape
    tm = ROW_TILE
    assert n % tm == 0
    row_spec = pl.BlockSpec((tm, d), lambda i: (i, 0))
    wspec = _const_spec(w_k.shape)
    gspec = _const_spec((1, d))
    return pl.pallas_call(
        _kvq_rows_kernel,
        grid=(n // tm,),
        in_specs=[row_spec, gspec, gspec, wspec, wspec, wspec],
        out_specs=[row_spec] * 3,
        out_shape=[jax.ShapeDtypeStruct((n, d), F32)] * 3,
        compiler_params=_params(1),
        name="kvq_rows",
    )(h, g_kv, g_q, w_k, w_v, w_q)


def _kvq_cols_kernel(h_ref, gkv_ref, gq_ref, wkt_ref, wvt_ref, wqt_ref,
                     kt_ref, vt_ref, kb_ref, vtb_ref, qtb_ref):
    xr = _rms_scale(h_ref[0])
    hkv = (xr * gkv_ref[...]).astype(BF16)
    hq = (xr * gq_ref[...]).astype(BF16)
    kt = _nt_dot(wkt_ref[...], hkv)
    kt_ref[0] = kt
    kb_ref[0] = kt.T.astype(BF16)
    vt = _nt_dot(wvt_ref[...], hkv)
    vt_ref[0] = vt
    vtb_ref[0] = vt.astype(BF16)
    qtb_ref[0] = (_nt_dot(wqt_ref[...], hq) * (HEAD_DIM ** -0.5)).astype(BF16)


def _kvq_cols_call(h, g_kv, g_q, w_kt, w_vt, w_qt):
    b, t_len, d = h.shape
    tm = ROW_TILE
    assert t_len % tm == 0
    row_spec = pl.BlockSpec((1, tm, d), lambda bi, i: (bi, i, 0))
    col_spec = pl.BlockSpec((1, d, tm), lambda bi, i: (bi, 0, i))
    wspec = _const_spec(w_kt.shape)
    gspec = _const_spec((1, d))
    cols = lambda dt: jax.ShapeDtypeStruct((b, d, t_len), dt)
    return pl.pallas_call(
        _kvq_cols_kernel,
        grid=(b, t_len // tm),
        in_specs=[row_spec, gspec, gspec, wspec, wspec, wspec],
        out_specs=[col_spec, col_spec, row_spec, col_spec, col_spec],
        out_shape=[cols(F32), cols(F32), jax.ShapeDtypeStruct((b, t_len, d), BF16), cols(BF16), cols(BF16)],
        compiler_params=_params(2),
        name="kvq_cols",
    )(h, g_kv, g_q, w_kt, w_vt, w_qt)


def _attn_kernel(bias_ref, qt_ref, k_ref, vt_ref, o_ref, acc_ref, z_ref, cum_ref):
    tq, tk = ATTN_TQ, ATTN_TK
    hb = pl.program_id(1)
    i = pl.program_id(2)
    row = lax.broadcasted_iota(jnp.int32, (tk, tq), 0)
    col = lax.broadcasted_iota(jnp.int32, (tk, tq), 1)
    triu = (lax.broadcasted_iota(jnp.int32, (tk, tk), 1)
            >= lax.broadcasted_iota(jnp.int32, (tk, tk), 0)).astype(BF16)
    heads = range(HEADS_PER_BLOCK)
    pair_w = 2 * HEAD_DIM
    pair_row = lax.broadcasted_iota(jnp.int32, (pair_w, tq), 0)
    ones = jnp.ones((tk, pair_w), BF16)
    qbs = []
    for h in heads:
        pair = qt_ref[0, (h // 2) * pair_w:(h // 2 + 1) * pair_w, :]
        qt = jnp.where(pair_row // HEAD_DIM == h % 2, pair, jnp.zeros_like(pair))
        bias = bias_ref[hb * HEADS_PER_BLOCK + h]
        bias_hi = bias.astype(BF16).astype(F32)
        bias_rows = jnp.where(pair_row == 0, bias_hi, jnp.where(pair_row == 1, bias - bias_hi, 0.0))
        qbs.append(jnp.concatenate([qt, bias_rows.astype(BF16)], axis=0))

    def park(start, key_offset, slot):
        for h in heads:
            k2 = k_ref[0, pl.ds(start, tk), (h // 2) * pair_w:(h // 2 + 1) * pair_w]
            z = jnp.dot(jnp.concatenate([k2, ones], axis=1), qbs[h], preferred_element_type=F32)
            if key_offset is not None:
                z = jnp.where(row + key_offset < col, z, MASKED_SCORE)
            z_ref[slot, h] = z
        sps = [(jnp.maximum(z_ref[slot, h], 0.0)
                + jnp.log(1.0 + jnp.exp2(jnp.abs(z_ref[slot, h]) * (-LOG2E)))).astype(BF16) for h in heads]
        for h in heads:
            cum_ref[slot, h] = jnp.dot(triu, sps[h], preferred_element_type=F32)

    def finish(start, slot, right_sums):
        avs = [jnp.exp(z_ref[slot, h] - cum_ref[slot, h]).astype(BF16) for h in heads]
        outs = [jnp.dot(vt_ref[0, h * HEAD_DIM:(h + 1) * HEAD_DIM, pl.ds(start, tk)], avs[h],
                        preferred_element_type=F32) for h in heads]
        new_sums = []
        for h in heads:
            acc_ref[h] += outs[h] * jnp.exp(-right_sums[h])
            new_sums.append(right_sums[h] + cum_ref[slot, h, 0:1, :])
        return tuple(new_sums)

    acc_ref[...] = jnp.zeros_like(acc_ref)
    sums = tuple(jnp.zeros((1, tq), F32) for _ in heads)
    base = i * tq
    park(pl.multiple_of(base + tk, tk), tk, 0)
    park(pl.multiple_of(base, tk), 0, 1)
    sums = finish(pl.multiple_of(base + tk, tk), 0, sums)

    def body(j, sums):
        start = pl.multiple_of(base - (2 * j + 1) * tk, tk)
        park(start, None, 0)
        sums = finish(pl.multiple_of(start + tk, tk), 1, sums)
        park(pl.multiple_of(start - tk, tk), None, 1)
        return finish(start, 0, sums)

    sums = lax.fori_loop(0, i, body, sums)
    finish(0, 1, sums)
    for h in heads:
        o_ref[0, :, h * HEAD_DIM:(h + 1) * HEAD_DIM] = acc_ref[h].T.astype(o_ref.dtype)


def _attn_call(qtb, kb, vtb, bias):
    b, t_len, d = kb.shape
    tq, tk = ATTN_TQ, ATTN_TK
    assert tq == 2 * tk, "the tile pipeline alternates two buffer slots per query tile"
    w = HEADS_PER_BLOCK * HEAD_DIM
    park_buffer = pltpu.VMEM((2, HEADS_PER_BLOCK, tk, tq), F32)
    return pl.pallas_call(
        _attn_kernel,
        grid=(b, d // w, t_len // tq),
        in_specs=[pl.BlockSpec(memory_space=pltpu.SMEM),
                  pl.BlockSpec((1, w, tq), lambda bi, hb, i: (bi, hb, i)),
                  pl.BlockSpec((1, t_len, w), lambda bi, hb, i: (bi, 0, hb)),
                  pl.BlockSpec((1, w, t_len), lambda bi, hb, i: (bi, hb, 0))],
        out_specs=pl.BlockSpec((1, tq, w), lambda bi, hb, i: (bi, i, hb)),
        out_shape=jax.ShapeDtypeStruct((b, t_len, d), BF16),
        scratch_shapes=[pltpu.VMEM((HEADS_PER_BLOCK, HEAD_DIM, tq), F32), park_buffer, park_buffer],
        compiler_params=_params(3),
        name="sb_prompt",
    )(bias, qtb, kb, vtb)


def _dec_kernel(pt_ref, qbd_ref, bias_ref, knew_ref, vnew_ref, *rest, n_new):
    del pt_ref
    pages = DEC_PAGES_PER_STEP
    k_refs, v_refs = rest[:pages], rest[pages:2 * pages]
    o_ref, acc_ref, carry_ref = rest[2 * pages:]
    p = pl.program_id(1)
    gw = DEC_GROUP * HEAD_DIM
    gr = DEC_GROUP * n_new
    n_groups = N_HEADS // DEC_GROUP
    page = k_refs[0].shape[3]
    bias = bias_ref[...]
    r16 = lax.broadcasted_iota(jnp.int32, (gr, 1), 0)
    own = (lax.broadcasted_iota(jnp.int32, (gr, gw), 0) // n_new
           == lax.broadcasted_iota(jnp.int32, (gr, gw), 1) // HEAD_DIM)

    @pl.when(p == 0)
    def _():
        for g in range(n_groups):
            rs = slice(g * gr, (g + 1) * gr)
            cs = slice(g * gw, (g + 1) * gw)
            qg = qbd_ref[0, rs, :].astype(F32)
            zs, sps = [], []
            for j in range(n_new):
                z = jnp.sum(qg * knew_ref[0, j:j + 1, cs], axis=1, keepdims=True) + bias[rs]
                zs.append(z)
                sps.append(jnp.where(j < r16 % n_new, _softplus(z), 0.0))
            cum = jnp.zeros((gr, 1), F32)
            acc = jnp.zeros((gr, gw), F32)
            for j in reversed(range(n_new)):
                cum = cum + sps[j]
                a = jnp.where(j < r16 % n_new, jnp.exp(zs[j] - cum), 0.0)
                acc = acc + a * vnew_ref[0, j:j + 1, cs]
            acc_ref[rs, :] = acc
            carry_ref[rs, :] = cum

    jj = lax.broadcasted_iota(jnp.int32, (page, page), 0)
    ss = lax.broadcasted_iota(jnp.int32, (page, page), 1)
    tri = (jj >= ss).astype(BF16)
    groups = [slice(g * gw, (g + 1) * gw) for g in range(n_groups)]
    rows = [slice(g * gr, (g + 1) * gr) for g in range(n_groups)]
    zs = []
    for k_ref in k_refs:
        kp = k_ref[0].reshape(N_HEADS * HEAD_DIM, page)
        zs.append(jnp.concatenate(
            [jnp.dot(qbd_ref[0, rows[g], :], kp[groups[g]].astype(BF16), preferred_element_type=F32)
             for g in range(n_groups)], axis=0) + bias)
    cums = [jnp.dot(_softplus(z).astype(BF16), tri, preferred_element_type=F32) for z in zs]
    avs = [jnp.exp(z - cum).astype(BF16) for z, cum in zip(zs, cums)]
    outs = []
    for a, v_ref in zip(avs, v_refs):
        vp = v_ref[0].reshape(N_HEADS * HEAD_DIM, page)
        outs.append([_nt_dot(a[rows[g]], vp[groups[g]].astype(BF16)) for g in range(n_groups)])
    carry = carry_ref[...]
    for out, cum in zip(outs, cums):
        scale = jnp.exp(-carry)
        for g in range(n_groups):
            acc_ref[rows[g], :] += out[g] * scale[rows[g]]
        carry = carry + cum[:, :1]
    carry_ref[...] = carry

    @pl.when(p == pl.num_programs(1) - 1)
    def _():
        for g in range(n_groups):
            mine = jnp.where(own, acc_ref[g * gr:(g + 1) * gr, :], 0.0)
            o = mine[0:n_new]
            for h4 in range(1, DEC_GROUP):
                o = o + mine[h4 * n_new:(h4 + 1) * n_new]
            o_ref[0, :, g * gw:(g + 1) * gw] = o.astype(o_ref.dtype)


def _dec_call(page_table, qbd, bias_col, k_new, v_new, cache_kt, cache_vt):
    db, nr, gw = qbd.shape
    n_new, d = k_new.shape[1:]
    n_pages = page_table.shape[1]
    pages = DEC_PAGES_PER_STEP
    pt = page_table.reshape(-1)
    seq_spec = lambda rows, cols: pl.BlockSpec((1, rows, cols), lambda b, p, pt_ref: (b, 0, 0))

    def page_spec(k):
        return pl.BlockSpec((1,) + cache_kt.shape[1:],
                            lambda b, p, pt_ref: (pt_ref[b * n_pages + n_pages - 1 - p * pages - k], 0, 0, 0))

    grid_spec = pltpu.PrefetchScalarGridSpec(
        num_scalar_prefetch=1,
        grid=(db, n_pages // pages),
        in_specs=[seq_spec(nr, gw), pl.BlockSpec((nr, 1), lambda b, p, pt_ref: (0, 0)),
                  seq_spec(n_new, d), seq_spec(n_new, d)]
        + [page_spec(k) for k in range(pages)] * 2,
        out_specs=seq_spec(n_new, d),
        scratch_shapes=[pltpu.VMEM((nr, gw), F32), pltpu.VMEM((nr, 1), F32)],
    )
    return pl.pallas_call(
        functools.partial(_dec_kernel, n_new=n_new),
        grid_spec=grid_spec,
        out_shape=jax.ShapeDtypeStruct((db, n_new, d), BF16),
        compiler_params=_params(2),
        name="sb_decode",
    )(pt, qbd, bias_col, k_new, v_new, *([cache_kt] * pages), *([cache_vt] * pages))


def kernel(x_prompt, x_sample, cache_k, cache_v, page_table, norm_mix_g, a_w_in, a_v_g, a_ws, a_bs, a_w_out,
           kv_norm_g, w_k, w_v, b_w_q, b_sb_bias, b_w_o, norm_ffn_g, w_gate, w_up, w_down, norm_final_g):
    bsz, seq, d = x_prompt.shape
    db, n_l, _ = x_sample.shape
    depth = norm_mix_g.shape[0]
    n_a = a_w_in.shape[0]
    n_pages = page_table.shape[1]
    assert depth == 2 and n_a == 1 and b_w_q.shape[0] == 1, "one SGU layer followed by one stick-breaking layer"
    assert d == N_HEADS * HEAD_DIM and SGU_CHUNK % n_l == 0 and n_pages % DEC_PAGES_PER_STEP == 0

    row = lambda g: g.reshape(1, d)
    bf = lambda w: w.astype(BF16)

    gd = d // SGU_GROUPS
    mix_p = a_ws[0]
    bsb_p = jnp.repeat(a_bs[0].T, gd, axis=1)
    reps = SGU_CHUNK // n_l
    eye = jnp.eye(reps, dtype=F32)
    mix_s = jnp.einsum('ab,gts->gatbs', eye, a_ws[0][:, :n_l, :n_l]).reshape(SGU_GROUPS, SGU_CHUNK, SGU_CHUNK)
    bsb_s = jnp.repeat(jnp.tile(a_bs[0][:, :n_l], (1, reps)).T, gd, axis=1)

    w_in, w_out = bf(a_w_in[0]), bf(a_w_out[0])
    xp = x_prompt.reshape(bsz * seq, d)
    xs = x_sample.reshape(db * n_l, d)
    (hp,) = _sgu_call(xp, row(norm_mix_g[0]), w_in, row(a_v_g[0]), mix_p, bsb_p, w_out, emit_v=False)
    hs, v_rows = _sgu_call(xs, row(norm_mix_g[0]), w_in, row(a_v_g[0]), mix_s, bsb_s, w_out, emit_v=True)
    wg0, wu0, wd0 = bf(w_gate[0]), bf(w_up[0]), bf(w_down[0])
    hp = _ffn_call(hp, None, None, row(norm_ffn_g[0]), wg0, wu0, wd0, None)
    hs = _ffn_call(hs, None, None, row(norm_ffn_g[0]), wg0, wu0, wd0, None)

    wq, wo = bf(b_w_q[0]), bf(b_w_o[0])
    kt_p, vt_p, kb_p, vtb_p, qtb_p = _kvq_cols_call(hp.reshape(bsz, seq, d), row(kv_norm_g), row(norm_mix_g[1]),
                                                    bf(w_k.T), bf(w_v.T), bf(b_w_q[0].T))
    k_s, v_s, q_s = _kvq_rows_call(hs, row(kv_norm_g), row(norm_mix_g[1]), bf(w_k), bf(w_v), wq)

    bias = b_sb_bias[0].astype(F32)
    o_p = _attn_call(qtb_p, kb_p, vtb_p, bias)

    n_groups = N_HEADS // DEC_GROUP
    q5 = q_s.reshape(db, n_l, n_groups, DEC_GROUP, HEAD_DIM)
    qbd = jnp.einsum('blghd,hk->bghlkd', q5, jnp.eye(DEC_GROUP, dtype=F32))
    qbd = qbd.reshape(db, N_HEADS * n_l, DEC_GROUP * HEAD_DIM).astype(BF16)
    bias_col = jnp.repeat(bias, n_l).reshape(N_HEADS * n_l, 1)
    cache_kt = jnp.transpose(cache_k, (0, 2, 3, 1))
    cache_vt = jnp.transpose(cache_v, (0, 2, 3, 1))
    o_s = _dec_call(page_table, qbd, bias_col, k_s.reshape(db, n_l, d), v_s.reshape(db, n_l, d),
                    cache_kt, cache_vt)

    wg1, wu1, wd1 = bf(w_gate[1]), bf(w_up[1]), bf(w_down[1])
    y_p = _ffn_call(hp, o_p.reshape(bsz * seq, d), wo, row(norm_ffn_g[1]), wg1, wu1, wd1, row(norm_final_g))
    y_s = _ffn_call(hs, o_s.reshape(db * n_l, d), wo, row(norm_ffn_g[1]), wg1, wu1, wd1, row(norm_final_g))

    heads = (N_HEADS, HEAD_DIM)
    to_bthd = lambda xt: jnp.transpose(xt.reshape(bsz, *heads, seq), (0, 3, 1, 2))
    return (y_p.reshape(bsz, seq, d), y_s.reshape(db, n_l, d),
            to_bthd(kt_p), to_bthd(vt_p),
            k_s.reshape(db, n_l, *heads), v_s.reshape(db, n_l, *heads),
            v_rows.reshape(n_a, db, n_l, d))
```
